```python
import jax, jax.numpy as jnp
from jax import lax
import numpy as np

D_MODEL = 2048
BATCH = 4
SEQ = 8192
DEPTH = 4
DEC_BATCH = 16
DEC_SEQ = 32
PAST_LEN = 1024

CHUNK = 64
N_MIXERS = 2
N_HGRN = (DEPTH + 1) // 2
N_ATTN = DEPTH // 2
H_A = 16
DK_A = D_MODEL // H_A
DV_A = D_MODEL // H_A
FORGET_DIM = H_A * DK_A
HGRN_BLOCK = 8
H_B = 16
HD_B = D_MODEL // H_B
N_PREV_CHUNKS = 8
ATTN_WINDOW = N_PREV_CHUNKS * CHUNK
BAND = ATTN_WINDOW + CHUNK
MAX_REL = 256
N_REL = MAX_REL + CHUNK
VAL_LO, VAL_HI = 2 * D_MODEL, 3 * D_MODEL
ALPHA = (2 * DEPTH) ** 0.25
BETA = (8 * DEPTH) ** -0.25
LN_EPS = 1e-5
RMS_EPS = 1e-6
LB_TINY = 1e-30
NEG_BIG = -1e30

kernel_name = "hybrid_hgrn2_chunkband_stream_step"

F32 = jnp.float32


def layer_norm(x, g, b):
    xf = x.astype(F32)
    mu = jnp.mean(xf, axis=-1, keepdims=True)
    var = jnp.mean(jnp.square(xf - mu), axis=-1, keepdims=True)
    return ((xf - mu) * lax.rsqrt(var + LN_EPS) * g.astype(F32) + b.astype(F32)).astype(x.dtype)


def hgrn_lower_bounds(lb_param):
    p = jax.nn.softmax(lb_param.astype(F32), axis=0)
    return jnp.cumsum(p, axis=0) - p[0]


def hgrn_inputs(proj, lb):
    B, T = proj.shape[:2]
    q, z, i, g = jnp.split(proj, [FORGET_DIM, 2 * FORGET_DIM, 2 * FORGET_DIM + D_MODEL], axis=-1)
    zf = z.astype(F32)
    log_lb = jnp.log(jnp.maximum(lb, LB_TINY))
    log_f = jnp.logaddexp(jax.nn.log_sigmoid(zf), log_lb + jax.nn.log_sigmoid(-zf))
    k = (1.0 - lb) * jax.nn.sigmoid(-zf)
    q = jax.nn.silu(q.astype(F32))
    hk = lambda t: t.reshape(B, T, H_A, DK_A)
    return hk(q), hk(k), i.astype(F32).reshape(B, T, H_A, DV_A), hk(log_f), g


def hgrn_prompt(q, k, v, log_f):
    B, S = q.shape[:2]
    nb = S // HGRN_BLOCK
    blk = lambda t: t.reshape(B, nb, HGRN_BLOCK, t.shape[2], t.shape[3])
    q, k, v, log_f = blk(q), blk(k), blk(v), blk(log_f)
    b = jnp.cumsum(log_f, axis=2)
    b_end = b[:, :, -1:]
    causal = jnp.tril(jnp.ones((HGRN_BLOCK, HGRN_BLOCK), dtype=bool))
    diff = b[:, :, :, None] - b[:, :, None, :]
    dec = jnp.exp(jnp.where(causal[:, :, None, None], diff, 0.0))
    a = jnp.einsum('bnthk,bnshk,bntshk->bnhts', q, k, dec)
    a = jnp.where(causal, a, 0.0)
    o_intra = jnp.einsum('bnhts,bnshv->bnthv', a, v)
    qe = q * jnp.exp(b)
    kd = k * jnp.exp(b_end - b)

    def step(state, xs):
        qe_n, kd_n, v_n, dec_n = xs
        o_inter = jnp.einsum('bthk,bhkv->bthv', qe_n, state)
        state = state * dec_n[..., None] + jnp.einsum('bshk,bshv->bhkv', kd_n, v_n)
        return state, o_inter

    s0 = jnp.zeros((B, H_A, DK_A, DV_A), F32)
    xs = (jnp.moveaxis(qe, 1, 0), jnp.moveaxis(kd, 1, 0), jnp.moveaxis(v, 1, 0),
          jnp.moveaxis(jnp.exp(b_end[:, :, 0]), 1, 0))
    s_fin, o_inter = lax.scan(step, s0, xs)
    o = o_intra + jnp.moveaxis(o_inter, 0, 1)
    return o.reshape(B, S, H_A, DV_A), s_fin


def hgrn_sample(q, k, v, log_f, s0):
    def step(state, xs):
        q_t, k_t, v_t, lf_t = xs
        state = state * jnp.exp(lf_t)[..., None] + k_t[..., None] * v_t[..., None, :]
        return state, jnp.einsum('bhk,bhkv->bhv', q_t, state)

    xs = tuple(jnp.moveaxis(t, 1, 0) for t in (q, k, v, log_f))
    s_fin, o = lax.scan(step, s0.astype(F32), xs)
    return jnp.moveaxis(o, 0, 1), s_fin


def hgrn_readout(o, g, norm_g):
    B, T = o.shape[:2]
    o = o * lax.rsqrt(jnp.mean(jnp.square(o), axis=-1, keepdims=True) + RMS_EPS)
    return o.reshape(B, T, D_MODEL) * norm_g.astype(F32) * jax.nn.silu(g.astype(F32))


def attn_inputs(proj):
    B, T = proj.shape[:2]
    q, k, v, g = jnp.split(proj, 4, axis=-1)
    hv = lambda t: t.reshape(B, T, H_B, HD_B)
    return hv(q), hv(k), hv(v), g


def rel_bias(table, n_q, n_k, offset):
    d = offset + jnp.arange(n_q)[:, None] - jnp.arange(n_k)[None, :]
    idx = jnp.clip(d, -(CHUNK - 1), MAX_REL) + (CHUNK - 1)
    return table[:, idx].astype(F32)


def attn_prompt(q, k, v, bias):
    B, S = q.shape[:2]
    n_chunks = S // CHUNK
    pad = ((0, 0), (ATTN_WINDOW, 0), (0, 0), (0, 0))
    kp, vp = jnp.pad(k, pad), jnp.pad(v, pad)
    band_pos = jnp.arange(BAND)
    scale = HD_B ** -0.5

    def one_chunk(c):
        start = c * CHUNK
        qc = lax.dynamic_slice_in_dim(q, start, CHUNK, axis=1)
        kc = lax.dynamic_slice_in_dim(kp, start, BAND, axis=1)
        vc = lax.dynamic_slice_in_dim(vp, start, BAND, axis=1)
        s = jnp.einsum('bqhd,bkhd->bhqk', qc, kc, preferred_element_type=F32) * scale + bias
        valid = (start - ATTN_WINDOW + band_pos) >= 0
        s = jnp.where(valid, s, NEG_BIG)
        p = jax.nn.softmax(s, axis=-1)
        return jnp.einsum('bhqk,bkhd->bqhd', p.astype(vc.dtype), vc)

    o = lax.map(one_chunk, jnp.arange(n_chunks))
    return jnp.moveaxis(o, 0, 1).reshape(B, S, H_B, HD_B)


def attn_sample(q, k_all, v_all, bias):
    s = jnp.einsum('bqhd,bkhd->bhqk', q, k_all, preferred_element_type=F32) * (HD_B ** -0.5) + bias
    p = jax.nn.softmax(s, axis=-1)
    return jnp.einsum('bhqk,bkhd->bqhd', p.astype(v_all.dtype), v_all)


def attn_readout(o, g):
    B, T = o.shape[:2]
    return o.reshape(B, T, D_MODEL).astype(F32) * jax.nn.silu(g.astype(F32))


def setup_inputs(seed: int = 0) -> dict:
    key = jax.random.key(seed)
    ks = jax.random.split(key, 12)
    kv_win = min(ATTN_WINDOW, PAST_LEN)
    x_prompt = jax.random.normal(ks[0], (BATCH, SEQ, D_MODEL), F32)
    x_sample = jax.random.normal(ks[1], (DEC_BATCH, DEC_SEQ, D_MODEL), F32)
    state_hgrn = 0.5 * jax.random.normal(ks[2], (N_HGRN, DEC_BATCH, H_A, DK_A, DV_A), F32)
    cache_attn_k = jax.random.normal(ks[3], (N_ATTN, DEC_BATCH, kv_win, H_B, HD_B), F32)
    cache_attn_v = BETA * jax.random.normal(ks[4], (N_ATTN, DEC_BATCH, kv_win, H_B, HD_B), F32)
    col_scale = jnp.ones((4 * D_MODEL,), F32).at[VAL_LO:VAL_HI].set(BETA)
    w_in = jax.random.normal(ks[5], (DEPTH, D_MODEL, 4 * D_MODEL), F32) * (D_MODEL ** -0.5) * col_scale
    w_out = jax.random.normal(ks[6], (DEPTH, D_MODEL, D_MODEL), F32) * (D_MODEL ** -0.5) * BETA
    ln_g = 1.0 + 0.02 * jax.random.normal(ks[7], (DEPTH, D_MODEL), F32)
    ln_b = 0.02 * jax.random.normal(ks[8], (DEPTH, D_MODEL), F32)
    hgrn_lb = 1.0 + 0.1 * jax.random.normal(ks[9], (N_HGRN, FORGET_DIM), F32)
    hgrn_norm_g = 1.0 + 0.02 * jax.random.normal(ks[10], (N_HGRN, D_MODEL), F32)
    attn_rel_bias = 0.1 * jax.random.normal(ks[11], (N_ATTN, H_B, N_REL), F32)
    return {"x_prompt": x_prompt, "x_sample": x_sample, "state_hgrn": state_hgrn,
            "cache_attn_k": cache_attn_k, "cache_attn_v": cache_attn_v,
            "w_in": w_in, "w_out": w_out, "ln_g": ln_g, "ln_b": ln_b,
            "hgrn_lb": hgrn_lb, "hgrn_norm_g": hgrn_norm_g, "attn_rel_bias": attn_rel_bias}


def reference(x_prompt, x_sample, state_hgrn, cache_attn_k, cache_attn_v,
              w_in, w_out, ln_g, ln_b, hgrn_lb, hgrn_norm_g, attn_rel_bias):
    lbs = hgrn_lower_bounds(hgrn_lb)
    kv_win = cache_attn_k.shape[2]
    dec_seq = x_sample.shape[1]
    prompt_win = min(ATTN_WINDOW, x_prompt.shape[1])
    hp, hs = x_prompt, x_sample
    st_p, st_s, kp_rows, vp_rows, ks_rows, vs_rows = [], [], [], [], [], []
    for layer in range(DEPTH):
        j = layer // N_MIXERS
        proj_p = hp @ w_in[layer]
        proj_s = hs @ w_in[layer]
        if layer % N_MIXERS == 0:
            q, k, v, lf, g = hgrn_inputs(proj_p, lbs[j])
            o, s_fin = hgrn_prompt(q, k, v, lf)
            mix_p = hgrn_readout(o, g, hgrn_norm_g[j])
            q, k, v, lf, g = hgrn_inputs(proj_s, lbs[j])
            o, s_new = hgrn_sample(q, k, v, lf, state_hgrn[j])
            mix_s = hgrn_readout(o, g, hgrn_norm_g[j])
            st_p.append(s_fin.astype(state_hgrn.dtype))
            st_s.append(s_new.astype(state_hgrn.dtype))
        else:
            table = attn_rel_bias[j]
            q, k, v, g = attn_inputs(proj_p)
            o = attn_prompt(q, k, v, rel_bias(table, CHUNK, BAND, ATTN_WINDOW))
            mix_p = attn_readout(o, g)
            kp_rows.append(k[:, -prompt_win:])
            vp_rows.append(v[:, -prompt_win:])
            q, k, v, g = attn_inputs(proj_s)
            k_all = jnp.concatenate([cache_attn_k[j].astype(k.dtype), k], axis=1)
            v_all = jnp.concatenate([cache_attn_v[j].astype(v.dtype), v], axis=1)
            o = attn_sample(q, k_all, v_all, rel_bias(table, dec_seq, kv_win + dec_seq, kv_win))
            mix_s = attn_readout(o, g)
            ks_rows.append(k)
            vs_rows.append(v)
        hp = layer_norm(ALPHA * hp + mix_p.astype(hp.dtype) @ w_out[layer], ln_g[layer], ln_b[layer])
        hs = layer_norm(ALPHA * hs + mix_s.astype(hs.dtype) @ w_out[layer], ln_g[layer], ln_b[layer])
    state_hgrn_prompt = jnp.stack(st_p)
    state_hgrn_sample = jnp.stack(st_s)
    cache_k_prompt = jnp.stack(kp_rows)
    cache_v_prompt = jnp.stack(vp_rows)
    cache_k_sample = jnp.stack(ks_rows)
    cache_v_sample = jnp.stack(vs_rows)
    return (hp, hs, state_hgrn_prompt, state_hgrn_sample,
            cache_k_prompt, cache_v_prompt, cache_k_sample, cache_v_sample)
```

```python
import functools

import numpy as np
import jax
import jax.numpy as jnp
from jax import lax
from jax.experimental import pallas as pl
from jax.experimental.pallas import tpu as pltpu

F32 = jnp.float32
BF16 = jnp.bfloat16

N_MIXERS = 2
HEADS = 16
CHUNK = 64
ATTN_WINDOW = 8 * CHUNK
MAX_REL = 256
LN_EPS = 1e-5
RMS_EPS = 1e-6
LB_TINY = 1e-30
NEG_BIG = -1e30

LANES = 128
HGRN_CHUNK = 128
ATTN_TQ = ATTN_WINDOW
VMEM_LIMIT = 48 * 1024 * 1024


def _silu(x):
    return x / (1.0 + jnp.exp(-x))


def _mm_kernel(x_ref, w_ref, o_ref):
    o_ref[...] = jnp.dot(x_ref[...], w_ref[...], preferred_element_type=F32).astype(o_ref.dtype)


def _in_proj(x16, w16, tm, tn):
    M, K = x16.shape
    N = w16.shape[1]
    assert M % tm == 0 and N % tn == 0
    return pl.pallas_call(
        _mm_kernel,
        out_shape=jax.ShapeDtypeStruct((M, N), F32),
        grid=(N // tn, M // tm),
        in_specs=[pl.BlockSpec((tm, K), lambda j, i: (i, 0)),
                  pl.BlockSpec((K, tn), lambda j, i: (0, j))],
        out_specs=pl.BlockSpec((tm, tn), lambda j, i: (i, j)),
        compiler_params=pltpu.CompilerParams(
            dimension_semantics=("arbitrary", "arbitrary"), vmem_limit_bytes=VMEM_LIMIT),
        name="in_proj",
    )(x16, w16)


def _out_ln_kernel(m_ref, w_ref, h_ref, g_ref, b_ref, o_ref, o16_ref, *, alpha):
    y = alpha * h_ref[...] + jnp.dot(m_ref[...], w_ref[...], preferred_element_type=F32)
    mu = jnp.mean(y, axis=-1, keepdims=True)
    yc = y - mu
    var = jnp.mean(yc * yc, axis=-1, keepdims=True)
    out = yc * lax.rsqrt(var + LN_EPS) * g_ref[...] + b_ref[...]
    o_ref[...] = out
    o16_ref[...] = out.astype(BF16)


def _out_ln(mix16, w16, h, g, b, alpha, tm):
    M, D = h.shape
    assert M % tm == 0
    row = lambda i: (i, 0)
    const = lambda i: (0, 0)
    return pl.pallas_call(
        functools.partial(_out_ln_kernel, alpha=alpha),
        out_shape=(jax.ShapeDtypeStruct((M, D), F32), jax.ShapeDtypeStruct((M, D), BF16)),
        grid=(M // tm,),
        in_specs=[pl.BlockSpec((tm, D), row), pl.BlockSpec((D, D), const),
                  pl.BlockSpec((tm, D), row), pl.BlockSpec((1, D), const), pl.BlockSpec((1, D), const)],
        out_specs=(pl.BlockSpec((tm, D), row), pl.BlockSpec((tm, D), row)),
        compiler_params=pltpu.CompilerParams(
            dimension_semantics=("arbitrary",), vmem_limit_bytes=VMEM_LIMIT),
        name="out_proj_ln",
    )(mix16, w16, h, g.reshape(1, D), b.reshape(1, D))


def _seg_bcast(p, n, row):
    C, L = p.shape
    if n >= 8:
        groups = C // (2 * n)
        x = p.reshape(groups, 2 * n, L)[:, n - 1:n, :]
        return jnp.broadcast_to(x, (groups, 2 * n, L)).reshape(C, L)
    r = row & (2 * n - 1)
    out = p
    for delta in range(-n, n):
        if delta == 0:
            continue
        cand = pltpu.roll(p, (-delta) % C, 0)
        out = jnp.where(r == n - 1 - delta, cand, out)
    return out


def _hgrn_kernel(*refs, layer_j, chunk, has_s0):
    if has_s0:
        lvl_ref, lb_ref, ng_ref, q_ref, z_ref, i_ref, g_ref, s0_ref, mix_ref, sfin_ref, st_ref = refs
    else:
        lvl_ref, lb_ref, ng_ref, q_ref, z_ref, i_ref, g_ref, mix_ref, sfin_ref, st_ref = refs
    C = chunk
    c = pl.program_id(2)

    @pl.when(c == 0)
    def _():
        if has_s0:
            st_ref[...] = s0_ref[0, 0].T
        else:
            st_ref[...] = jnp.zeros_like(st_ref)

    lbp = lb_ref[...].astype(F32)
    pe = jnp.exp(lbp - jnp.max(lbp, axis=0, keepdims=True))
    prob = pe / jnp.sum(pe, axis=0, keepdims=True)
    cs = prob[0:1]
    for i in range(1, layer_j + 1):
        cs = cs + prob[i:i + 1]
    lb = cs - prob[0:1]
    log_lb = jnp.log(jnp.maximum(lb, LB_TINY))

    z = z_ref[0]
    e = jnp.exp(-jnp.abs(z))
    l1p = jnp.log1p(e)
    ls_pos = jnp.minimum(z, 0.0) - l1p
    ls_neg = jnp.minimum(-z, 0.0) - l1p
    y = log_lb + ls_neg
    a = jnp.maximum(ls_pos, y) + jnp.log1p(jnp.exp(-jnp.abs(ls_pos - y)))
    kk = (1.0 - lb) * (jnp.where(z >= 0, e, 1.0) / (1.0 + e))
    qs = _silu(q_ref[0])
    v16 = i_ref[0].astype(BF16)

    row = lax.broadcasted_iota(jnp.int32, (C, LANES), 0)
    lvl = lvl_ref[...]
    dims_nt = (((1,), (1,)), ((), ()))

    amat = lax.dot_general(qs.astype(BF16), kk.astype(BF16), dims_nt, preferred_element_type=F32)
    amat = jnp.where(lvl == 0, amat, 0.0)
    p = a
    n, k = 1, 1
    while n < C:
        bn = _seg_bcast(p, n, row)
        right = (row & n) != 0
        ex = jnp.exp(jnp.where(right, p, bn - p))
        an = lax.dot_general((qs * ex).astype(BF16), (kk * ex).astype(BF16), dims_nt,
                             preferred_element_type=F32)
        amat = jnp.where(lvl == k, an, amat)
        p = p + jnp.where(right, bn, 0.0)
        n, k = 2 * n, k + 1

    b_end = p[C - 1:C, :]
    qe = (qs * jnp.exp(p)).astype(BF16)
    kd = (kk * jnp.exp(b_end - p)).astype(BF16)
    st = st_ref[...]
    o = jnp.dot(amat.astype(BF16), v16, preferred_element_type=F32)
    o = o + lax.dot_general(qe, st.astype(BF16), dims_nt, preferred_element_type=F32)
    st_new = st * jnp.exp(b_end) + jnp.dot(v16.T, kd, preferred_element_type=F32)
    st_ref[...] = st_new

    ms = jnp.mean(o * o, axis=-1, keepdims=True)
    mix = o * lax.rsqrt(ms + RMS_EPS) * ng_ref[...].astype(F32) * _silu(g_ref[0])
    mix_ref[0] = mix.astype(mix_ref.dtype)

    @pl.when(c == pl.num_programs(2) - 1)
    def _():
        sfin_ref[0, 0] = st_new.T


def _level_ids(C):
    t = np.arange(C)[:, None]
    s = np.arange(C)[None, :]
    x = t ^ s
    lv = np.full((C, C), -1, np.int32)
    lv[x == 0] = 0
    n, k = 1, 1
    while n < C:
        lv[(x >= n) & (x < 2 * n) & (t > s)] = k
        n, k = 2 * n, k + 1
    return jnp.asarray(lv)


def _hgrn_mixer(proj, hgrn_lb, norm_g, layer_j, chunk, s0=None):
    nb, T, D4 = proj.shape
    D = D4 // 4
    H = HEADS
    assert D // H == LANES and T % chunk == 0
    n_layers = hgrn_lb.shape[0]
    has_s0 = s0 is not None
    col = lambda off: (lambda b, h, c: (b, c, off * H + h))
    in_specs = [
        pl.BlockSpec((chunk, chunk), lambda b, h, c: (0, 0)),
        pl.BlockSpec((n_layers, LANES), lambda b, h, c: (0, h)),
        pl.BlockSpec((1, LANES), lambda b, h, c: (0, h)),
        pl.BlockSpec((1, chunk, LANES), col(0)),
        pl.BlockSpec((1, chunk, LANES), col(1)),
        pl.BlockSpec((1, chunk, LANES), col(2)),
        pl.BlockSpec((1, chunk, LANES), col(3)),
    ]
    args = [_level_ids(chunk), hgrn_lb, norm_g.reshape(1, D), proj, proj, proj, proj]
    if has_s0:
        in_specs.append(pl.BlockSpec((1, 1, LANES, LANES), lambda b, h, c: (b, h, 0, 0)))
        args.append(s0)
    return pl.pallas_call(
        functools.partial(_hgrn_kernel, layer_j=layer_j, chunk=chunk, has_s0=has_s0),
        out_shape=(jax.ShapeDtypeStruct((nb, T, D), BF16),
                   jax.ShapeDtypeStruct((nb, H, LANES, LANES), F32)),
        grid=(nb, H, T // chunk),
        in_specs=in_specs,
        out_specs=(pl.BlockSpec((1, chunk, LANES), lambda b, h, c: (b, c, h)),
                   pl.BlockSpec((1, 1, LANES, LANES), lambda b, h, c: (b, h, 0, 0))),
        scratch_shapes=[pltpu.VMEM((LANES, LANES), F32)],
        compiler_params=pltpu.CompilerParams(
            dimension_semantics=("arbitrary", "arbitrary", "arbitrary"), vmem_limit_bytes=VMEM_LIMIT),
        name="hgrn_sample" if has_s0 else "hgrn_prompt",
    )(*args)


def _toeplitz_bias(table, rows, cols):
    H, n_rel = table.shape
    assert n_rel == MAX_REL + CHUNK
    d_hi = ATTN_WINDOW + rows - 1
    d_lo = ATTN_WINDOW - (cols - 1)
    left = d_hi - MAX_REL
    right = -(CHUNK - 1) - d_lo
    assert left >= 0 and right >= 0
    g = jnp.pad(table[:, ::-1], ((0, 0), (left, right)), mode="edge")
    n = rows + cols - 1
    g = jnp.pad(g, ((0, 0), (0, 1)))
    flat = jnp.tile(g, (1, rows))[:, rows - 1: rows - 1 + rows * n]
    return flat.reshape(H, rows, n)[:, :, :cols]


def _softmax_parts(parts):
    m = functools.reduce(jnp.maximum, [jnp.max(s, axis=-1, keepdims=True) for s in parts])
    ps = [jnp.exp(s - m) for s in parts]
    den = functools.reduce(lambda x, y: x + y, [jnp.sum(p, axis=-1, keepdims=True) for p in ps])
    return ps, den


def _attn_prompt_kernel(bias_ref, q_ref, kp_ref, ko_ref, vp_ref, vo_ref, g_ref, mix_ref, *, scale):
    i = pl.program_id(2)
    tq = q_ref.shape[1]
    dims_nt = (((1,), (1,)), ((), ()))
    q16 = q_ref[0].astype(BF16)
    s_prev = lax.dot_general(q16, kp_ref[0].astype(BF16), dims_nt, preferred_element_type=F32)
    s_own = lax.dot_general(q16, ko_ref[0].astype(BF16), dims_nt, preferred_element_type=F32)
    s_prev = s_prev * scale + bias_ref[0, :, :tq]
    s_own = s_own * scale + bias_ref[0, :, tq:]
    s_prev = jnp.where(i > 0, s_prev, NEG_BIG)
    (p_prev, p_own), den = _softmax_parts([s_prev, s_own])
    o = jnp.dot(p_prev.astype(BF16), vp_ref[0].astype(BF16), preferred_element_type=F32)
    o = o + jnp.dot(p_own.astype(BF16), vo_ref[0].astype(BF16), preferred_element_type=F32)
    mix_ref[0] = (o / den * _silu(g_ref[0])).astype(mix_ref.dtype)


def _attn_prompt(proj, bias):
    B, S, D4 = proj.shape
    D = D4 // 4
    H = HEADS
    tq = ATTN_TQ
    assert D // H == LANES and S % tq == 0
    own = lambda off: (lambda b, h, i: (b, i, off * H + h))
    prev = lambda off: (lambda b, h, i: (b, jnp.maximum(i - 1, 0), off * H + h))
    blk = (1, tq, LANES)
    return pl.pallas_call(
        functools.partial(_attn_prompt_kernel, scale=float(LANES) ** -0.5),
        out_shape=jax.ShapeDtypeStruct((B, S, D), BF16),
        grid=(B, H, S // tq),
        in_specs=[pl.BlockSpec((1, tq, 2 * tq), lambda b, h, i: (h, 0, 0)),
                  pl.BlockSpec(blk, own(0)),
                  pl.BlockSpec(blk, prev(1)), pl.BlockSpec(blk, own(1)),
                  pl.BlockSpec(blk, prev(2)), pl.BlockSpec(blk, own(2)),
                  pl.BlockSpec(blk, own(3))],
        out_specs=pl.BlockSpec(blk, lambda b, h, i: (b, i, h)),
        compiler_params=pltpu.CompilerParams(
            dimension_semantics=("arbitrary", "arbitrary", "arbitrary"), vmem_limit_bytes=VMEM_LIMIT),
        name="attn_prompt",
    )(bias, proj, proj, proj, proj, proj, proj)


def _attn_sample_kernel(bc_ref, bn_ref, q_ref, kn_ref, vn_ref, g_ref, kc_ref, vc_ref, mix_ref, *, scale):
    dims_nt = (((1,), (1,)), ((), ()))
    q16 = q_ref[0].astype(BF16)
    s_c = lax.dot_general(q16, kc_ref[0].astype(BF16), dims_nt, preferred_element_type=F32)
    s_n = lax.dot_general(q16, kn_ref[0].astype(BF16), dims_nt, preferred_element_type=F32)
    s_c = s_c * scale + bc_ref[0]
    s_n = s_n * scale + bn_ref[0]
    (p_c, p_n), den = _softmax_parts([s_c, s_n])
    o = jnp.dot(p_c.astype(BF16), vc_ref[0].astype(BF16), preferred_element_type=F32)
    o = o + jnp.dot(p_n.astype(BF16), vn_ref[0].astype(BF16), preferred_element_type=F32)
    mix_ref[0] = (o / den * _silu(g_ref[0])).astype(mix_ref.dtype)


def _attn_sample(proj, cache_k, cache_v, bias_c, bias_n):
    nb, T, D4 = proj.shape
    D = D4 // 4
    H = HEADS
    W = cache_k.shape[1]
    new = lambda off: (lambda b, h: (b, 0, off * H + h))
    blk = (1, T, LANES)
    cblk = (1, W, LANES)
    return pl.pallas_call(
        functools.partial(_attn_sample_kernel, scale=float(LANES) ** -0.5),
        out_shape=jax.ShapeDtypeStruct((nb, T, D), BF16),
        grid=(nb, H),
        in_specs=[pl.BlockSpec((1, T, W), lambda b, h: (h, 0, 0)),
                  pl.BlockSpec((1, T, T), lambda b, h: (h, 0, 0)),
                  pl.BlockSpec(blk, new(0)), pl.BlockSpec(blk, new(1)),
                  pl.BlockSpec(blk, new(2)), pl.BlockSpec(blk, new(3)),
                  pl.BlockSpec(cblk, lambda b, h: (b, 0, h)),
                  pl.BlockSpec(cblk, lambda b, h: (b, 0, h))],
        out_specs=pl.BlockSpec(blk, lambda b, h: (b, 0, h)),
        compiler_params=pltpu.CompilerParams(
            dimension_semantics=("arbitrary", "arbitrary"), vmem_limit_bytes=VMEM_LIMIT),
        name="attn_sample",
    )(bias_c, bias_n, proj, proj, proj, proj, cache_k, cache_v)


def _row_tile(m, cap):
    t = cap
    while m % t:
        t //= 2
    return t


def kernel(x_prompt, x_sample, state_hgrn, cache_attn_k, cache_attn_v, w_in, w_out, ln_g, ln_b,
           hgrn_lb, hgrn_norm_g, attn_rel_bias):
    B, S, D = x_prompt.shape
    nb, T, _ = x_sample.shape
    depth = w_in.shape[0]
    H = HEADS
    hd = D // H
    alpha = (2 * depth) ** 0.25
    kv_win = cache_attn_k.shape[2]
    prompt_win = min(ATTN_WINDOW, S)
    assert kv_win == ATTN_WINDOW

    w_in16 = w_in.astype(BF16)
    w_out16 = w_out.astype(BF16)
    hp = x_prompt.reshape(B * S, D)
    hs = x_sample.reshape(nb * T, D)
    hp16 = hp.astype(BF16)
    hs16 = hs.astype(BF16)
    tm_p = _row_tile(B * S, 1024)
    tm_s = _row_tile(nb * T, 512)

    qc = np.arange(ATTN_TQ)[:, None] // CHUNK + ATTN_TQ // CHUNK
    kc = np.arange(2 * ATTN_TQ)[None, :] // CHUNK
    band_ok = jnp.asarray((kc <= qc) & (kc >= qc - ATTN_WINDOW // CHUNK))

    st_p, st_s, kp_rows, vp_rows, ks_rows, vs_rows = [], [], [], [], [], []
    for layer in range(depth):
        j = layer // N_MIXERS
        proj_p = _in_proj(hp16, w_in16[layer], tm_p, 1024).reshape(B, S, 4 * D)
        proj_s = _in_proj(hs16, w_in16[layer], tm_s, 1024).reshape(nb, T, 4 * D)
        if layer % N_MIXERS == 0:
            mix_p, s_fin = _hgrn_mixer(proj_p, hgrn_lb, hgrn_norm_g[j], j, min(HGRN_CHUNK, S))
            mix_s, s_new = _hgrn_mixer(proj_s, hgrn_lb, hgrn_norm_g[j], j, T, s0=state_hgrn[j])
            st_p.append(s_fin.astype(state_hgrn.dtype))
            st_s.append(s_new.astype(state_hgrn.dtype))
        else:
            toep = _toeplitz_bias(attn_rel_bias[j].astype(F32), ATTN_TQ, 2 * ATTN_TQ)
            bias_p = jnp.where(band_ok, toep, NEG_BIG)
            mix_p = _attn_prompt(proj_p, bias_p)
            mix_s = _attn_sample(proj_s, cache_attn_k[j].reshape(nb, kv_win, D),
                                 cache_attn_v[j].reshape(nb, kv_win, D),
                                 toep[:, :T, :kv_win], toep[:, :T, kv_win:kv_win + T])
            kp_rows.append(proj_p[:, S - prompt_win:, D:2 * D].reshape(B, prompt_win, H, hd))
            vp_rows.append(proj_p[:, S - prompt_win:, 2 * D:3 * D].reshape(B, prompt_win, H, hd))
            ks_rows.append(proj_s[:, :, D:2 * D].reshape(nb, T, H, hd))
            vs_rows.append(proj_s[:, :, 2 * D:3 * D].reshape(nb, T, H, hd))
        hp, hp16 = _out_ln(mix_p.reshape(B * S, D), w_out16[layer], hp, ln_g[layer], ln_b[layer], alpha,
                           _row_tile(B * S, 256))
        hs, hs16 = _out_ln(mix_s.reshape(nb * T, D), w_out16[layer], hs, ln_g[layer], ln_b[layer], alpha,
                           _row_tile(nb * T, 256))
    return (hp.reshape(B, S, D), hs.reshape(nb, T, D),
            jnp.stack(st_p), jnp.stack(st_s),
            jnp.stack(kp_rows), jnp.stack(vp_rows), jnp.stack(ks_rows), jnp.stack(vs_rows))
```

```python
import functools

import numpy as np
import jax
import jax.numpy as jnp
from jax import lax
from jax.experimental import pallas as pl
from jax.experimental.pallas import tpu as pltpu

F32 = jnp.float32
BF16 = jnp.bfloat16

N_MIXERS = 2
HEADS = 16
CHUNK = 64
ATTN_WINDOW = 8 * CHUNK
MAX_REL = 256
LN_EPS = 1e-5
RMS_EPS = 1e-6
LB_TINY = 1e-30
NEG_BIG = -1e30

LANES = 128
SUBLANES = 8
BF16_ROWS = 16
HGRN_CHUNK = 128
HGRN_HEADS_PROMPT = 8
HGRN_HEADS_SAMPLE = 16
ATTN_TQ = ATTN_WINDOW
ATTN_SUB = 128
ATTN_KW = ATTN_SUB + ATTN_WINDOW
ATTN_HEADS = 8
VMEM_LIMIT = 48 * 1024 * 1024
LOG2E = 1.4426950408889634

_DIMS_NT = (((1,), (1,)), ((), ()))


def _silu(x):
    h = 0.5 * x
    return h * jnp.tanh(h) + h


def _dot_nt(x, y):
    return lax.dot_general(x, y, _DIMS_NT, preferred_element_type=F32)


def _mm_kernel(x_ref, w_ref, o_ref, w16_ref):
    @pl.when(pl.program_id(1) == 0)
    def _():
        w16_ref[...] = w_ref[...].astype(BF16)

    o_ref[...] = jnp.dot(x_ref[...], w16_ref[...], preferred_element_type=F32).astype(o_ref.dtype)


def _in_proj(x16, w, layer, tm, tn, out_dtype):
    M, K = x16.shape
    N = w.shape[2]
    assert M % tm == 0 and N % tn == 0
    return pl.pallas_call(
        _mm_kernel,
        out_shape=jax.ShapeDtypeStruct((M, N), out_dtype),
        grid=(N // tn, M // tm),
        in_specs=[pl.BlockSpec((tm, K), lambda j, i: (i, 0)),
                  pl.BlockSpec((None, K, tn), lambda j, i: (layer, 0, j))],
        out_specs=pl.BlockSpec((tm, tn), lambda j, i: (i, j)),
        scratch_shapes=[pltpu.VMEM((K, tn), BF16)],
        compiler_params=pltpu.CompilerParams(
            dimension_semantics=("arbitrary", "arbitrary"), vmem_limit_bytes=VMEM_LIMIT),
        name="in_proj",
    )(x16, w)


def _out_ln_kernel(m_ref, w_ref, h_ref, g_ref, b_ref, o_ref, o16_ref, *, alpha):
    y = alpha * h_ref[...] + jnp.dot(m_ref[...], w_ref[...], preferred_element_type=F32)
    mu = jnp.mean(y, axis=-1, keepdims=True)
    yc = y - mu
    var = jnp.mean(yc * yc, axis=-1, keepdims=True)
    out = yc * lax.rsqrt(var + LN_EPS) * g_ref[...] + b_ref[...]
    o_ref[...] = out
    o16_ref[...] = out.astype(BF16)


def _out_ln(mix16, w16, layer, h, g, b, alpha, tm):
    M, D = h.shape
    assert M % tm == 0
    row = lambda i: (i, 0)
    const = lambda i: (0, 0)
    return pl.pallas_call(
        functools.partial(_out_ln_kernel, alpha=alpha),
        out_shape=(jax.ShapeDtypeStruct((M, D), F32), jax.ShapeDtypeStruct((M, D), BF16)),
        grid=(M // tm,),
        in_specs=[pl.BlockSpec((tm, D), row), pl.BlockSpec((None, D, D), lambda i: (layer, 0, 0)),
                  pl.BlockSpec((tm, D), row), pl.BlockSpec((1, D), const), pl.BlockSpec((1, D), const)],
        out_specs=(pl.BlockSpec((tm, D), row), pl.BlockSpec((tm, D), row)),
        compiler_params=pltpu.CompilerParams(
            dimension_semantics=("arbitrary",), vmem_limit_bytes=VMEM_LIMIT),
        name="out_proj_ln",
    )(mix16, w16, h, g.reshape(1, D), b.reshape(1, D))


def _hgrn_head(z, q, vin, g, lb, ng, st, masks, sub, C):
    log_lb = jnp.log(jnp.maximum(lb, LB_TINY))
    e = jnp.exp(-jnp.abs(z))
    one_e = 1.0 + e
    l1p = jnp.log(one_e)
    ls_pos = jnp.minimum(z, 0.0) - l1p
    ls_neg = jnp.minimum(-z, 0.0) - l1p
    y = log_lb + ls_neg
    a = jnp.maximum(ls_pos, y) + jnp.log(1.0 + jnp.exp(-jnp.abs(ls_pos - y)))
    kk = (1.0 - lb) * (jnp.where(z >= 0, e, 1.0) / one_e)
    qs = _silu(q)
    qs16 = qs.astype(BF16)
    kk16 = kk.astype(BF16)
    v16 = vin.astype(BF16)

    V = C // SUBLANES
    shape3 = (V, SUBLANES, LANES)
    p3 = a.reshape(shape3)
    qs3 = qs.reshape(shape3)
    kk3 = kk.reshape(shape3)
    parts = [_dot_nt(qs16, kk16)]
    for n in (1, 2, 4):
        if n == 1:
            bn = jnp.where((sub & 1) != 0, pltpu.roll(p3, 1, 1), p3)
        elif n == 2:
            bn = jnp.where(sub < 4, jnp.broadcast_to(p3[:, 1:2], shape3), jnp.broadcast_to(p3[:, 5:6], shape3))
        else:
            bn = jnp.broadcast_to(p3[:, 3:4], shape3)
        right = (sub & n) != 0
        ex = jnp.exp(jnp.where(right, p3, bn - p3))
        qn = (qs3 * ex).reshape(C, LANES).astype(BF16)
        kn = (kk3 * ex).reshape(C, LANES).astype(BF16)
        parts.append(_dot_nt(qn, kn))
        p3 = p3 + jnp.where(right, bn, 0.0)
    p = p3.reshape(C, LANES)

    n = SUBLANES
    while n < C:
        q_rows, k_rows, p_rows = [], [], []
        for lo in range(0, C, 2 * n):
            mid, hi = lo + n, lo + 2 * n
            bn = p[mid - 1:mid]
            k_left = kk[lo:mid] * jnp.exp(bn - p[lo:mid])
            q_right = qs[mid:hi] * jnp.exp(p[mid:hi])
            if n >= BF16_ROWS:
                q_rows += [qs16[lo:mid], q_right.astype(BF16)]
                k_rows += [k_left.astype(BF16), kk16[mid:hi]]
            else:
                q_rows += [qs[lo:mid], q_right]
                k_rows += [k_left, kk[mid:hi]]
            p_rows += [p[lo:mid], p[mid:hi] + bn]
        parts.append(_dot_nt(jnp.concatenate(q_rows, axis=0).astype(BF16),
                             jnp.concatenate(k_rows, axis=0).astype(BF16)))
        p = jnp.concatenate(p_rows, axis=0)
        n *= 2

    amat = jnp.where(masks[0], parts[0], 0.0)
    for k in (1, 2, 3):
        amat = jnp.where(masks[k], parts[k], amat)
    n, k = SUBLANES, 4
    while n < C:
        rows = []
        for lo in range(0, C, 2 * n):
            mid, hi = lo + n, lo + 2 * n
            rows.append(amat[lo:mid])
            rows.append(jnp.where(masks[k][mid:hi], parts[k][mid:hi], amat[mid:hi]))
        amat = jnp.concatenate(rows, axis=0)
        n, k = 2 * n, k + 1

    b_end = p[C - 1:C, :]
    qe = (qs * jnp.exp(p)).astype(BF16)
    kd = (kk * jnp.exp(b_end - p)).astype(BF16)
    o = jnp.dot(amat.astype(BF16), v16, preferred_element_type=F32) + _dot_nt(qe, st.astype(BF16))
    st_new = st * jnp.exp(b_end) + jnp.dot(v16.T, kd, preferred_element_type=F32)

    ms = jnp.mean(o * o, axis=-1, keepdims=True)
    mix = o * lax.rsqrt(ms + RMS_EPS) * ng * _silu(g)
    return mix, st_new


def _hgrn_kernel(*refs, layer_j, chunk, heads, has_s0):
    if has_s0:
        lvl_ref, lb_ref, ng_ref, q_ref, z_ref, i_ref, g_ref, s0_ref, mix_ref, sfin_ref, st_ref = refs
    else:
        lvl_ref, lb_ref, ng_ref, q_ref, z_ref, i_ref, g_ref, mix_ref, sfin_ref, st_ref = refs
    C = chunk
    c = pl.program_id(2)

    @pl.when(c == 0)
    def _():
        for h in range(heads):
            if has_s0:
                st_ref[h] = s0_ref[0, h].T
            else:
                st_ref[h] = jnp.zeros((LANES, LANES), F32)

    lbp = lb_ref[...].astype(F32)
    pe = jnp.exp(lbp - jnp.max(lbp, axis=0, keepdims=True))
    prob = pe / jnp.sum(pe, axis=0, keepdims=True)
    cs = prob[0:1]
    for i in range(1, layer_j + 1):
        cs = cs + prob[i:i + 1]
    lb = cs - prob[0:1]
    ng = ng_ref[...].astype(F32)

    lvl = lvl_ref[...]
    n_levels = C.bit_length()
    masks = [lvl == k for k in range(n_levels)]
    sub = lax.broadcasted_iota(jnp.int32, (1, SUBLANES, LANES), 1)

    for h in range(heads):
        sl = slice(h * LANES, (h + 1) * LANES)
        mix, st_new = _hgrn_head(z_ref[0, :, sl], q_ref[0, :, sl], i_ref[0, :, sl], g_ref[0, :, sl],
                                 lb[:, sl], ng[:, sl], st_ref[h], masks, sub, C)
        st_ref[h] = st_new
        mix_ref[0, :, sl] = mix.astype(mix_ref.dtype)

    @pl.when(c == pl.num_programs(2) - 1)
    def _():
        for h in range(heads):
            sfin_ref[0, h] = st_ref[h].T


def _level_ids(C):
    t = np.arange(C)[:, None]
    s = np.arange(C)[None, :]
    x = t ^ s
    lv = np.full((C, C), -1, np.int32)
    lv[x == 0] = 0
    n, k = 1, 1
    while n < C:
        lv[(x >= n) & (x < 2 * n) & (t > s)] = k
        n, k = 2 * n, k + 1
    return jnp.asarray(lv)


def _hgrn_mixer(proj, hgrn_lb, norm_g, layer_j, chunk, heads, s0=None):
    nb, T, D4 = proj.shape
    D = D4 // 4
    H = HEADS
    assert D // H == LANES and T % chunk == 0 and H % heads == 0 and chunk % BF16_ROWS == 0
    n_layers = hgrn_lb.shape[0]
    has_s0 = s0 is not None
    hg = H // heads
    w = heads * LANES
    col = lambda off: (lambda b, h, c: (b, c, off * hg + h))
    st_blk = pl.BlockSpec((1, heads, LANES, LANES), lambda b, h, c: (b, h, 0, 0))
    in_specs = [
        pl.BlockSpec((chunk, chunk), lambda b, h, c: (0, 0)),
        pl.BlockSpec((n_layers, w), lambda b, h, c: (0, h)),
        pl.BlockSpec((1, w), lambda b, h, c: (0, h)),
        pl.BlockSpec((1, chunk, w), col(0)),
        pl.BlockSpec((1, chunk, w), col(1)),
        pl.BlockSpec((1, chunk, w), col(2)),
        pl.BlockSpec((1, chunk, w), col(3)),
    ]
    args = [_level_ids(chunk), hgrn_lb, norm_g.reshape(1, D), proj, proj, proj, proj]
    if has_s0:
        in_specs.append(st_blk)
        args.append(s0)
    return pl.pallas_call(
        functools.partial(_hgrn_kernel, layer_j=layer_j, chunk=chunk, heads=heads, has_s0=has_s0),
        out_shape=(jax.ShapeDtypeStruct((nb, T, D), BF16),
                   jax.ShapeDtypeStruct((nb, H, LANES, LANES), F32)),
        grid=(nb, hg, T // chunk),
        in_specs=in_specs,
        out_specs=(pl.BlockSpec((1, chunk, w), lambda b, h, c: (b, c, h)), st_blk),
        scratch_shapes=[pltpu.VMEM((heads, LANES, LANES), F32)],
        compiler_params=pltpu.CompilerParams(
            dimension_semantics=("arbitrary", "arbitrary", "arbitrary"), vmem_limit_bytes=VMEM_LIMIT),
        name="hgrn_sample" if has_s0 else "hgrn_prompt",
    )(*args)


def _toeplitz_bias(table, rows, cols):
    H, n_rel = table.shape
    assert n_rel == MAX_REL + CHUNK
    d_hi = ATTN_WINDOW + rows - 1
    d_lo = ATTN_WINDOW - (cols - 1)
    left = d_hi - MAX_REL
    right = -(CHUNK - 1) - d_lo
    assert left >= 0 and right >= 0
    g = jnp.pad(table[:, ::-1], ((0, 0), (left, right)), mode="edge")
    n = rows + cols - 1
    g = jnp.pad(g, ((0, 0), (0, 1)))
    flat = jnp.tile(g, (1, rows))[:, rows - 1: rows - 1 + rows * n]
    return flat.reshape(H, rows, n)[:, :, :cols]


def _softmax2_pv(scores, values):
    m = functools.reduce(jnp.maximum, [jnp.max(s, axis=-1, keepdims=True) for s in scores])
    ps = [jnp.exp2(s - m) for s in scores]
    den = functools.reduce(lambda x, y: x + y, [jnp.sum(p, axis=-1, keepdims=True) for p in ps])
    o = functools.reduce(lambda x, y: x + y,
                         [jnp.dot(p.astype(BF16), v, preferred_element_type=F32) for p, v in zip(ps, values)])
    return o * (1.0 / den)


def _attn_prompt_kernel(bias_ref, q_ref, k_ref, v_ref, g_ref, mix_ref, kprev_ref, vprev_ref, *, qscale, heads):
    i = pl.program_id(2)
    tq = q_ref.shape[1]

    @pl.when(i == 0)
    def _():
        kprev_ref[...] = jnp.zeros_like(kprev_ref)
        vprev_ref[...] = jnp.zeros_like(vprev_ref)

    has_prev = i > 0
    for h in range(heads):
        sl = slice(h * LANES, (h + 1) * LANES)
        bias = bias_ref[h]
        q16 = (q_ref[0, :, sl].astype(F32) * qscale).astype(BF16)
        kcat = jnp.concatenate([kprev_ref[:, sl], k_ref[0, :, sl].astype(BF16)], axis=0)
        vcat = jnp.concatenate([vprev_ref[:, sl], v_ref[0, :, sl].astype(BF16)], axis=0)
        s_all = _dot_nt(q16, kcat)
        p_rows, inv_rows = [], []
        for r0 in range(0, tq, ATTN_SUB):
            n_prev = ATTN_WINDOW - r0
            s = s_all[r0:r0 + ATTN_SUB, r0:r0 + ATTN_KW] + bias
            s = jnp.concatenate([jnp.where(has_prev, s[:, :n_prev], NEG_BIG), s[:, n_prev:]], axis=1)
            p = jnp.exp2(s - jnp.max(s, axis=-1, keepdims=True))
            inv_rows.append(1.0 / jnp.sum(p, axis=-1, keepdims=True))
            pieces = [p.astype(BF16)]
            if r0:
                pieces.insert(0, jnp.zeros((ATTN_SUB, r0), BF16))
            if 2 * tq - ATTN_KW - r0:
                pieces.append(jnp.zeros((ATTN_SUB, 2 * tq - ATTN_KW - r0), BF16))
            p_rows.append(jnp.concatenate(pieces, axis=1))
        o = jnp.dot(jnp.concatenate(p_rows, axis=0), vcat, preferred_element_type=F32)
        o = o * jnp.concatenate(inv_rows, axis=0)
        mix_ref[0, :, sl] = (o * _silu(g_ref[0, :, sl].astype(F32))).astype(mix_ref.dtype)

    kprev_ref[...] = k_ref[0].astype(BF16)
    vprev_ref[...] = v_ref[0].astype(BF16)


def _attn_prompt(proj, bias2):
    B, S, D4 = proj.shape
    D = D4 // 4
    H = HEADS
    tq = ATTN_TQ
    heads = ATTN_HEADS
    hg = H // heads
    w = heads * LANES
    assert D // H == LANES and S % tq == 0 and H % heads == 0
    col = lambda off: (lambda b, h, i: (b, i, off * hg + h))
    blk = (1, tq, w)
    return pl.pallas_call(
        functools.partial(_attn_prompt_kernel, qscale=float(LANES) ** -0.5 * LOG2E, heads=heads),
        out_shape=jax.ShapeDtypeStruct((B, S, D), BF16),
        grid=(B, hg, S // tq),
        in_specs=[pl.BlockSpec((heads, ATTN_SUB, ATTN_KW), lambda b, h, i: (h, 0, 0)),
                  pl.BlockSpec(blk, col(0)), pl.BlockSpec(blk, col(1)),
                  pl.BlockSpec(blk, col(2)), pl.BlockSpec(blk, col(3))],
        out_specs=pl.BlockSpec(blk, lambda b, h, i: (b, i, h)),
        scratch_shapes=[pltpu.VMEM((tq, w), BF16), pltpu.VMEM((tq, w), BF16)],
        compiler_params=pltpu.CompilerParams(
            dimension_semantics=("arbitrary", "arbitrary", "arbitrary"), vmem_limit_bytes=VMEM_LIMIT),
        name="attn_prompt",
    )(bias2, proj, proj, proj, proj)


def _attn_sample_kernel(bc_ref, bn_ref, q_ref, kn_ref, vn_ref, g_ref, kc_ref, vc_ref, mix_ref, *, qscale, heads):
    for h in range(heads):
        sl = slice(h * LANES, (h + 1) * LANES)
        q16 = (q_ref[0, :, sl].astype(F32) * qscale).astype(BF16)
        s_c = _dot_nt(q16, kc_ref[0, :, sl].astype(BF16)) + bc_ref[h]
        s_n = _dot_nt(q16, kn_ref[0, :, sl].astype(BF16)) + bn_ref[h]
        o = _softmax2_pv([s_c, s_n], [vc_ref[0, :, sl].astype(BF16), vn_ref[0, :, sl].astype(BF16)])
        mix_ref[0, :, sl] = (o * _silu(g_ref[0, :, sl].astype(F32))).astype(mix_ref.dtype)


def _attn_sample(proj, cache_k, cache_v, layer_j, bias_c2, bias_n2):
    nb, T, D4 = proj.shape
    D = D4 // 4
    H = HEADS
    W = cache_k.shape[2]
    new = lambda off: (lambda b: (b, 0, off))
    blk = (1, T, D)
    cblk = (None, 1, W, D)
    return pl.pallas_call(
        functools.partial(_attn_sample_kernel, qscale=float(LANES) ** -0.5 * LOG2E, heads=H),
        out_shape=jax.ShapeDtypeStruct((nb, T, D), BF16),
        grid=(nb,),
        in_specs=[pl.BlockSpec((H, T, W), lambda b: (0, 0, 0)),
                  pl.BlockSpec((H, T, T), lambda b: (0, 0, 0)),
                  pl.BlockSpec(blk, new(0)), pl.BlockSpec(blk, new(1)),
                  pl.BlockSpec(blk, new(2)), pl.BlockSpec(blk, new(3)),
                  pl.BlockSpec(cblk, lambda b: (layer_j, b, 0, 0)),
                  pl.BlockSpec(cblk, lambda b: (layer_j, b, 0, 0))],
        out_specs=pl.BlockSpec(blk, lambda b: (b, 0, 0)),
        compiler_params=pltpu.CompilerParams(
            dimension_semantics=("arbitrary",), vmem_limit_bytes=VMEM_LIMIT),
        name="attn_sample",
    )(bias_c2, bias_n2, proj, proj, proj, proj, cache_k, cache_v)


def _row_tile(m, cap):
    t = cap
    while m % t:
        t //= 2
    return t


def kernel(x_prompt, x_sample, state_hgrn, cache_attn_k, cache_attn_v, w_in, w_out, ln_g, ln_b,
           hgrn_lb, hgrn_norm_g, attn_rel_bias):
    B, S, D = x_prompt.shape
    nb, T, _ = x_sample.shape
    depth = w_in.shape[0]
    H = HEADS
    hd = D // H
    alpha = (2 * depth) ** 0.25
    kv_win = cache_attn_k.shape[2]
    prompt_win = min(ATTN_WINDOW, S)
    assert kv_win == ATTN_WINDOW

    w_out16 = w_out.astype(BF16)
    hp = x_prompt.reshape(B * S, D)
    hs = x_sample.reshape(nb * T, D)
    hp16 = hp.astype(BF16)
    hs16 = hs.astype(BF16)
    tm_p = _row_tile(B * S, 1024)
    tm_s = _row_tile(nb * T, 512)
    cache_k = cache_attn_k.reshape(cache_attn_k.shape[0], nb, kv_win, D)
    cache_v = cache_attn_v.reshape(cache_attn_v.shape[0], nb, kv_win, D)

    qc = (np.arange(ATTN_SUB)[:, None] + ATTN_WINDOW) // CHUNK
    kc = np.arange(ATTN_KW)[None, :] // CHUNK
    band_ok = jnp.asarray((kc <= qc) & (kc >= qc - ATTN_WINDOW // CHUNK))

    st_p, st_s, kp_rows, vp_rows, ks_rows, vs_rows = [], [], [], [], [], []
    for layer in range(depth):
        j = layer // N_MIXERS
        is_hgrn = layer % N_MIXERS == 0
        proj_dtype = F32 if is_hgrn else BF16
        proj_p = _in_proj(hp16, w_in, layer, tm_p, 1024, proj_dtype).reshape(B, S, 4 * D)
        proj_s = _in_proj(hs16, w_in, layer, tm_s, 1024, proj_dtype).reshape(nb, T, 4 * D)
        if is_hgrn:
            mix_p, s_fin = _hgrn_mixer(proj_p, hgrn_lb, hgrn_norm_g[j], j, min(HGRN_CHUNK, S),
                                       HGRN_HEADS_PROMPT)
            mix_s, s_new = _hgrn_mixer(proj_s, hgrn_lb, hgrn_norm_g[j], j, T, HGRN_HEADS_SAMPLE,
                                       s0=state_hgrn[j])
            st_p.append(s_fin.astype(state_hgrn.dtype))
            st_s.append(s_new.astype(state_hgrn.dtype))
        else:
            toep2 = _toeplitz_bias(attn_rel_bias[j].astype(F32), ATTN_SUB, ATTN_KW) * LOG2E
            mix_p = _attn_prompt(proj_p, jnp.where(band_ok, toep2, NEG_BIG))
            mix_s = _attn_sample(proj_s, cache_k, cache_v, j,
                                 toep2[:, :T, :kv_win], toep2[:, :T, kv_win:kv_win + T])
            out_dt = x_prompt.dtype
            kp_rows.append(proj_p[:, S - prompt_win:, D:2 * D].astype(out_dt).reshape(B, prompt_win, H, hd))
            vp_rows.append(proj_p[:, S - prompt_win:, 2 * D:3 * D].astype(out_dt).reshape(B, prompt_win, H, hd))
            ks_rows.append(proj_s[:, :, D:2 * D].astype(out_dt).reshape(nb, T, H, hd))
            vs_rows.append(proj_s[:, :, 2 * D:3 * D].astype(out_dt).reshape(nb, T, H, hd))
        hp, hp16 = _out_ln(mix_p.reshape(B * S, D), w_out16, layer, hp, ln_g[layer], ln_b[layer], alpha,
                           _row_tile(B * S, 256))
        hs, hs16 = _out_ln(mix_s.reshape(nb * T, D), w_out16, layer, hs, ln_g[layer], ln_b[layer], alpha,
                           _row_tile(nb * T, 256))
    return (hp.reshape(B, S, D), hs.reshape(nb, T, D),
            jnp.stack(st_p), jnp.stack(st_s),
            jnp.stack(kp_rows), jnp.stack(vp_rows), jnp.stack(ks_rows), jnp.stack(vs_rows))
```

```python
import functools

import numpy as np
import jax
import jax.numpy as jnp
from jax import lax
from jax.experimental import pallas as pl
from jax.experimental.pallas import tpu as pltpu

F32 = jnp.float32
BF16 = jnp.bfloat16

N_MIXERS = 2
HEADS = 16
CHUNK = 64
ATTN_WINDOW = 8 * CHUNK
MAX_REL = 256
LN_EPS = 1e-5
RMS_EPS = 1e-6
LB_TINY = 1e-30
NEG_BIG = -1e30

LANES = 128
SUBLANES = 8
BF16_ROWS = 16
HGRN_CHUNK = 128
HGRN_HEADS_PROMPT = 8
HGRN_HEADS_SAMPLE = 16
ATTN_TQ = ATTN_WINDOW
ATTN_SUB = 128
ATTN_KW = ATTN_SUB + ATTN_WINDOW
ATTN_HEADS = 8
OUT_TM = 512
OUT_SUB = 128
VMEM_LIMIT = 48 * 1024 * 1024
LOG2E = 1.4426950408889634

_DIMS_NT = (((1,), (1,)), ((), ()))


def _silu(x):
    h = 0.5 * x
    return h * jnp.tanh(h) + h


def _dot_nt(x, y):
    return lax.dot_general(x, y, _DIMS_NT, preferred_element_type=F32)


def _mm_kernel(xp_ref, xs_ref, w_ref, op_ref, os_ref, w16_ref, *, n_prompt):
    i = pl.program_id(1)

    @pl.when(i == 0)
    def _():
        w16_ref[...] = w_ref[...].astype(BF16)

    @pl.when(i < n_prompt)
    def _():
        op_ref[...] = jnp.dot(xp_ref[...], w16_ref[...], preferred_element_type=F32).astype(op_ref.dtype)

    @pl.when(i == n_prompt)
    def _():
        os_ref[...] = jnp.dot(xs_ref[...], w16_ref[...], preferred_element_type=F32).astype(os_ref.dtype)


def _in_proj(xp16, xs16, w, layer, tm, tn, out_dtype):
    Mp, K = xp16.shape
    Ms = xs16.shape[0]
    N = w.shape[2]
    assert Mp % tm == 0 and N % tn == 0
    n_p = Mp // tm
    last = n_p - 1
    return pl.pallas_call(
        functools.partial(_mm_kernel, n_prompt=n_p),
        out_shape=(jax.ShapeDtypeStruct((Mp, N), out_dtype), jax.ShapeDtypeStruct((Ms, N), out_dtype)),
        grid=(N // tn, n_p + 1),
        in_specs=[pl.BlockSpec((tm, K), lambda j, i: (jnp.minimum(i, last), 0)),
                  pl.BlockSpec((Ms, K), lambda j, i: (0, 0)),
                  pl.BlockSpec((None, K, tn), lambda j, i: (layer, 0, j))],
        out_specs=(pl.BlockSpec((tm, tn), lambda j, i: (jnp.minimum(i, last), j)),
                   pl.BlockSpec((Ms, tn), lambda j, i: (0, j))),
        scratch_shapes=[pltpu.VMEM((K, tn), BF16)],
        compiler_params=pltpu.CompilerParams(
            dimension_semantics=("arbitrary", "arbitrary"), vmem_limit_bytes=VMEM_LIMIT),
        name="in_proj",
    )(xp16, xs16, w)


def _out_ln_kernel(m_ref, w_ref, h_ref, g_ref, b_ref, o_ref, o16_ref, *, alpha, sub):
    for r0 in range(0, m_ref.shape[0], sub):
        rows = slice(r0, r0 + sub)
        y = alpha * h_ref[rows] + jnp.dot(m_ref[rows], w_ref[...], preferred_element_type=F32)
        mu = jnp.mean(y, axis=-1, keepdims=True)
        yc = y - mu
        var = jnp.mean(yc * yc, axis=-1, keepdims=True)
        out = yc * lax.rsqrt(var + LN_EPS) * g_ref[...] + b_ref[...]
        o_ref[rows] = out
        o16_ref[rows] = out.astype(BF16)


def _out_ln(mix16, w16, layer, h, g, b, alpha, tm):
    M, D = h.shape
    assert M % tm == 0
    row = lambda i: (i, 0)
    const = lambda i: (0, 0)
    return pl.pallas_call(
        functools.partial(_out_ln_kernel, alpha=alpha, sub=min(tm, OUT_SUB)),
        out_shape=(jax.ShapeDtypeStruct((M, D), F32), jax.ShapeDtypeStruct((M, D), BF16)),
        grid=(M // tm,),
        in_specs=[pl.BlockSpec((tm, D), row),
                  pl.BlockSpec((None, D, D), lambda i: (layer, 0, 0), pipeline_mode=pl.Buffered(1)),
                  pl.BlockSpec((tm, D), row), pl.BlockSpec((1, D), const), pl.BlockSpec((1, D), const)],
        out_specs=(pl.BlockSpec((tm, D), row), pl.BlockSpec((tm, D), row)),
        compiler_params=pltpu.CompilerParams(
            dimension_semantics=("arbitrary",), vmem_limit_bytes=VMEM_LIMIT),
        name="out_proj_ln",
    )(mix16, w16, h, g.reshape(1, D), b.reshape(1, D))


def _hgrn_head(z, q, vin, g, lb, ng, st, masks, sub, C):
    lb_c = jnp.maximum(lb, LB_TINY)
    e = jnp.exp(-jnp.abs(z))
    pos = z >= 0
    inv = 1.0 / (1.0 + e)
    a = jnp.log(jnp.where(pos, 1.0 + lb_c * e, e + lb_c) * inv)
    kk = (1.0 - lb) * (jnp.where(pos, e, 1.0) * inv)
    qs = _silu(q)
    qs16 = qs.astype(BF16)
    kk16 = kk.astype(BF16)
    v16 = vin.astype(BF16)

    V = C // SUBLANES
    shape3 = (V, SUBLANES, LANES)
    p3 = a.reshape(shape3)
    qs3 = qs.reshape(shape3)
    kk3 = kk.reshape(shape3)
    parts = [_dot_nt(qs16, kk16)]
    for n in (1, 2, 4):
        if n == 1:
            bn = jnp.where((sub & 1) != 0, pltpu.roll(p3, 1, 1), p3)
        elif n == 2:
            bn = jnp.where(sub < 4, jnp.broadcast_to(p3[:, 1:2], shape3), jnp.broadcast_to(p3[:, 5:6], shape3))
        else:
            bn = jnp.broadcast_to(p3[:, 3:4], shape3)
        right = (sub & n) != 0
        ex = jnp.exp(jnp.where(right, p3, bn - p3))
        qn = (qs3 * ex).reshape(C, LANES).astype(BF16)
        kn = (kk3 * ex).reshape(C, LANES).astype(BF16)
        parts.append(_dot_nt(qn, kn))
        p3 = p3 + jnp.where(right, bn, 0.0)
    p = p3.reshape(C, LANES)

    n = SUBLANES
    while n < C:
        q_rows, k_rows, p_rows = [], [], []
        for lo in range(0, C, 2 * n):
            mid, hi = lo + n, lo + 2 * n
            bn = p[mid - 1:mid]
            k_left = kk[lo:mid] * jnp.exp(bn - p[lo:mid])
            q_right = qs[mid:hi] * jnp.exp(p[mid:hi])
            if n >= BF16_ROWS:
                q_rows += [qs16[lo:mid], q_right.astype(BF16)]
                k_rows += [k_left.astype(BF16), kk16[mid:hi]]
            else:
                q_rows += [qs[lo:mid], q_right]
                k_rows += [k_left, kk[mid:hi]]
            p_rows += [p[lo:mid], p[mid:hi] + bn]
        parts.append(_dot_nt(jnp.concatenate(q_rows, axis=0).astype(BF16),
                             jnp.concatenate(k_rows, axis=0).astype(BF16)))
        p = jnp.concatenate(p_rows, axis=0)
        n *= 2

    amat = jnp.where(masks[0], parts[0], 0.0)
    for k in (1, 2, 3):
        amat = jnp.where(masks[k], parts[k], amat)
    n, k = SUBLANES, 4
    while n < C:
        rows = []
        for lo in range(0, C, 2 * n):
            mid, hi = lo + n, lo + 2 * n
            rows.append(amat[lo:mid])
            rows.append(jnp.where(masks[k][mid:hi], parts[k][mid:hi], amat[mid:hi]))
        amat = jnp.concatenate(rows, axis=0)
        n, k = 2 * n, k + 1

    b_end = p[C - 1:C, :]
    qe = (qs * jnp.exp(p)).astype(BF16)
    kd = (kk * jnp.exp(b_end - p)).astype(BF16)
    o = jnp.dot(amat.astype(BF16), v16, preferred_element_type=F32) + _dot_nt(qe, st.astype(BF16))
    st_new = st * jnp.exp(b_end) + jnp.dot(v16.T, kd, preferred_element_type=F32)

    ms = jnp.mean(o * o, axis=-1, keepdims=True)
    mix = o * lax.rsqrt(ms + RMS_EPS) * ng * _silu(g)
    return mix, st_new


def _hgrn_kernel(*refs, layer_j, chunk, heads, has_s0):
    if has_s0:
        lvl_ref, lb_ref, ng_ref, q_ref, z_ref, i_ref, g_ref, s0_ref, mix_ref, sfin_ref, st_ref = refs
    else:
        lvl_ref, lb_ref, ng_ref, q_ref, z_ref, i_ref, g_ref, mix_ref, sfin_ref, st_ref = refs
    C = chunk
    c = pl.program_id(2)

    @pl.when(c == 0)
    def _():
        for h in range(heads):
            if has_s0:
                st_ref[h] = s0_ref[0, h].T
            else:
                st_ref[h] = jnp.zeros((LANES, LANES), F32)

    lbp = lb_ref[...].astype(F32)
    pe = jnp.exp(lbp - jnp.max(lbp, axis=0, keepdims=True))
    prob = pe / jnp.sum(pe, axis=0, keepdims=True)
    cs = prob[0:1]
    for i in range(1, layer_j + 1):
        cs = cs + prob[i:i + 1]
    lb = cs - prob[0:1]
    ng = ng_ref[...].astype(F32)

    lvl = lvl_ref[...]
    n_levels = C.bit_length()
    masks = [lvl == k for k in range(n_levels)]
    sub = lax.broadcasted_iota(jnp.int32, (1, SUBLANES, LANES), 1)

    for h in range(heads):
        sl = slice(h * LANES, (h + 1) * LANES)
        mix, st_new = _hgrn_head(z_ref[0, :, sl], q_ref[0, :, sl], i_ref[0, :, sl], g_ref[0, :, sl],
                                 lb[:, sl], ng[:, sl], st_ref[h], masks, sub, C)
        st_ref[h] = st_new
        mix_ref[0, :, sl] = mix.astype(mix_ref.dtype)

    @pl.when(c == pl.num_programs(2) - 1)
    def _():
        for h in range(heads):
            sfin_ref[0, h] = st_ref[h].T


def _level_ids(C):
    t = np.arange(C)[:, None]
    s = np.arange(C)[None, :]
    x = t ^ s
    lv = np.full((C, C), -1, np.int32)
    lv[x == 0] = 0
    n, k = 1, 1
    while n < C:
        lv[(x >= n) & (x < 2 * n) & (t > s)] = k
        n, k = 2 * n, k + 1
    return jnp.asarray(lv)


def _hgrn_mixer(proj, hgrn_lb, norm_g, layer_j, chunk, heads, s0=None):
    nb, T, D4 = proj.shape
    D = D4 // 4
    H = HEADS
    assert D // H == LANES and T % chunk == 0 and H % heads == 0 and chunk % BF16_ROWS == 0
    n_layers = hgrn_lb.shape[0]
    has_s0 = s0 is not None
    hg = H // heads
    w = heads * LANES
    col = lambda off: (lambda b, h, c: (b, c, off * hg + h))
    st_blk = pl.BlockSpec((1, heads, LANES, LANES), lambda b, h, c: (b, h, 0, 0))
    in_specs = [
        pl.BlockSpec((chunk, chunk), lambda b, h, c: (0, 0)),
        pl.BlockSpec((n_layers, w), lambda b, h, c: (0, h)),
        pl.BlockSpec((1, w), lambda b, h, c: (0, h)),
        pl.BlockSpec((1, chunk, w), col(0)),
        pl.BlockSpec((1, chunk, w), col(1)),
        pl.BlockSpec((1, chunk, w), col(2)),
        pl.BlockSpec((1, chunk, w), col(3)),
    ]
    args = [_level_ids(chunk), hgrn_lb, norm_g.reshape(1, D), proj, proj, proj, proj]
    if has_s0:
        in_specs.append(st_blk)
        args.append(s0)
    return pl.pallas_call(
        functools.partial(_hgrn_kernel, layer_j=layer_j, chunk=chunk, heads=heads, has_s0=has_s0),
        out_shape=(jax.ShapeDtypeStruct((nb, T, D), BF16),
                   jax.ShapeDtypeStruct((nb, H, LANES, LANES), F32)),
        grid=(nb, hg, T // chunk),
        in_specs=in_specs,
        out_specs=(pl.BlockSpec((1, chunk, w), lambda b, h, c: (b, c, h)), st_blk),
        scratch_shapes=[pltpu.VMEM((heads, LANES, LANES), F32)],
        compiler_params=pltpu.CompilerParams(
            dimension_semantics=("arbitrary", "arbitrary", "arbitrary"), vmem_limit_bytes=VMEM_LIMIT),
        name="hgrn_sample" if has_s0 else "hgrn_prompt",
    )(*args)


def _toeplitz_bias(table, rows, cols):
    H, n_rel = table.shape
    assert n_rel == MAX_REL + CHUNK
    d_hi = ATTN_WINDOW + rows - 1
    d_lo = ATTN_WINDOW - (cols - 1)
    left = d_hi - MAX_REL
    right = -(CHUNK - 1) - d_lo
    assert left >= 0 and right >= 0
    g = jnp.pad(table[:, ::-1], ((0, 0), (left, right)), mode="edge")
    n = rows + cols - 1
    g = jnp.pad(g, ((0, 0), (0, 1)))
    flat = jnp.tile(g, (1, rows))[:, rows - 1: rows - 1 + rows * n]
    return flat.reshape(H, rows, n)[:, :, :cols]


def _softmax2_pv(scores, values):
    m = functools.reduce(jnp.maximum, [jnp.max(s, axis=-1, keepdims=True) for s in scores])
    ps = [jnp.exp2(s - m) for s in scores]
    den = functools.reduce(lambda x, y: x + y, [jnp.sum(p, axis=-1, keepdims=True) for p in ps])
    o = functools.reduce(lambda x, y: x + y,
                         [jnp.dot(p.astype(BF16), v, preferred_element_type=F32) for p, v in zip(ps, values)])
    return o * (1.0 / den)


def _attn_prompt_kernel(bias_ref, q_ref, k_ref, v_ref, g_ref, mix_ref, kprev_ref, vprev_ref, *, qscale, heads):
    i = pl.program_id(2)
    tq = q_ref.shape[1]

    @pl.when(i == 0)
    def _():
        kprev_ref[...] = jnp.zeros_like(kprev_ref)
        vprev_ref[...] = jnp.zeros_like(vprev_ref)

    has_prev = i > 0
    for h in range(heads):
        sl = slice(h * LANES, (h + 1) * LANES)
        bias = bias_ref[h]
        q16 = (q_ref[0, :, sl].astype(F32) * qscale).astype(BF16)
        kcat = jnp.concatenate([kprev_ref[:, sl], k_ref[0, :, sl].astype(BF16)], axis=0)
        vcat = jnp.concatenate([vprev_ref[:, sl], v_ref[0, :, sl].astype(BF16)], axis=0)
        s_all = _dot_nt(q16, kcat)
        p_rows, inv_rows = [], []
        for r0 in range(0, tq, ATTN_SUB):
            n_prev = ATTN_WINDOW - r0
            s = s_all[r0:r0 + ATTN_SUB, r0:r0 + ATTN_KW] + bias
            s = jnp.concatenate([jnp.where(has_prev, s[:, :n_prev], NEG_BIG), s[:, n_prev:]], axis=1)
            p = jnp.exp2(s - jnp.max(s, axis=-1, keepdims=True))
            inv_rows.append(1.0 / jnp.sum(p, axis=-1, keepdims=True))
            pieces = [p.astype(BF16)]
            if r0:
                pieces.insert(0, jnp.zeros((ATTN_SUB, r0), BF16))
            if 2 * tq - ATTN_KW - r0:
                pieces.append(jnp.zeros((ATTN_SUB, 2 * tq - ATTN_KW - r0), BF16))
            p_rows.append(jnp.concatenate(pieces, axis=1))
        o = jnp.dot(jnp.concatenate(p_rows, axis=0), vcat, preferred_element_type=F32)
        o = o * jnp.concatenate(inv_rows, axis=0)
        mix_ref[0, :, sl] = (o * _silu(g_ref[0, :, sl].astype(F32))).astype(mix_ref.dtype)

    kprev_ref[...] = k_ref[0].astype(BF16)
    vprev_ref[...] = v_ref[0].astype(BF16)


def _attn_prompt(proj, bias2):
    B, S, D4 = proj.shape
    D = D4 // 4
    H = HEADS
    tq = ATTN_TQ
    heads = ATTN_HEADS
    hg = H // heads
    w = heads * LANES
    assert D // H == LANES and S % tq == 0 and H % heads == 0
    col = lambda off: (lambda b, h, i: (b, i, off * hg + h))
    blk = (1, tq, w)
    return pl.pallas_call(
        functools.partial(_attn_prompt_kernel, qscale=float(LANES) ** -0.5 * LOG2E, heads=heads),
        out_shape=jax.ShapeDtypeStruct((B, S, D), BF16),
        grid=(B, hg, S // tq),
        in_specs=[pl.BlockSpec((heads, ATTN_SUB, ATTN_KW), lambda b, h, i: (h, 0, 0)),
                  pl.BlockSpec(blk, col(0)), pl.BlockSpec(blk, col(1)),
                  pl.BlockSpec(blk, col(2)), pl.BlockSpec(blk, col(3))],
        out_specs=pl.BlockSpec(blk, lambda b, h, i: (b, i, h)),
        scratch_shapes=[pltpu.VMEM((tq, w), BF16), pltpu.VMEM((tq, w), BF16)],
        compiler_params=pltpu.CompilerParams(
            dimension_semantics=("arbitrary", "arbitrary", "arbitrary"), vmem_limit_bytes=VMEM_LIMIT),
        name="attn_prompt",
    )(bias2, proj, proj, proj, proj)


def _attn_sample_kernel(bc_ref, bn_ref, q_ref, kn_ref, vn_ref, g_ref, kc_ref, vc_ref, mix_ref, *, qscale, heads):
    win = kc_ref.shape[1] // heads
    for h in range(heads):
        sl = slice(h * LANES, (h + 1) * LANES)
        kc = kc_ref[0, pl.ds(h, win, stride=heads), :].astype(BF16)
        vc = vc_ref[0, pl.ds(h, win, stride=heads), :].astype(BF16)
        q16 = (q_ref[0, :, sl].astype(F32) * qscale).astype(BF16)
        s_c = _dot_nt(q16, kc) + bc_ref[h]
        s_n = _dot_nt(q16, kn_ref[0, :, sl].astype(BF16)) + bn_ref[h]
        o = _softmax2_pv([s_c, s_n], [vc, vn_ref[0, :, sl].astype(BF16)])
        mix_ref[0, :, sl] = (o * _silu(g_ref[0, :, sl].astype(F32))).astype(mix_ref.dtype)


def _attn_sample(proj, cache_k, cache_v, layer_j, bias_c2, bias_n2):
    nb, T, D4 = proj.shape
    D = D4 // 4
    H = HEADS
    W = cache_k.shape[2] // H
    new = lambda off: (lambda b: (b, 0, off))
    blk = (1, T, D)
    cblk = (None, 1, W * H, LANES)
    return pl.pallas_call(
        functools.partial(_attn_sample_kernel, qscale=float(LANES) ** -0.5 * LOG2E, heads=H),
        out_shape=jax.ShapeDtypeStruct((nb, T, D), BF16),
        grid=(nb,),
        in_specs=[pl.BlockSpec((H, T, W), lambda b: (0, 0, 0)),
                  pl.BlockSpec((H, T, T), lambda b: (0, 0, 0)),
                  pl.BlockSpec(blk, new(0)), pl.BlockSpec(blk, new(1)),
                  pl.BlockSpec(blk, new(2)), pl.BlockSpec(blk, new(3)),
                  pl.BlockSpec(cblk, lambda b: (layer_j, b, 0, 0)),
                  pl.BlockSpec(cblk, lambda b: (layer_j, b, 0, 0))],
        out_specs=pl.BlockSpec(blk, lambda b: (b, 0, 0)),
        compiler_params=pltpu.CompilerParams(
            dimension_semantics=("arbitrary",), vmem_limit_bytes=VMEM_LIMIT),
        name="attn_sample",
    )(bias_c2, bias_n2, proj, proj, proj, proj, cache_k, cache_v)


def _row_tile(m, cap):
    t = cap
    while m % t:
        t //= 2
    return t


def kernel(x_prompt, x_sample, state_hgrn, cache_attn_k, cache_attn_v, w_in, w_out, ln_g, ln_b,
           hgrn_lb, hgrn_norm_g, attn_rel_bias):
    B, S, D = x_prompt.shape
    nb, T, _ = x_sample.shape
    depth = w_in.shape[0]
    H = HEADS
    hd = D // H
    alpha = (2 * depth) ** 0.25
    kv_win = cache_attn_k.shape[2]
    prompt_win = min(ATTN_WINDOW, S)
    assert kv_win == ATTN_WINDOW

    w_out16 = w_out.astype(BF16)
    hp = x_prompt.reshape(B * S, D)
    hs = x_sample.reshape(nb * T, D)
    hp16 = hp.astype(BF16)
    hs16 = hs.astype(BF16)
    tm_p = _row_tile(B * S, 1024)
    cache_k = cache_attn_k.reshape(cache_attn_k.shape[0], nb, kv_win * H, hd)
    cache_v = cache_attn_v.reshape(cache_attn_v.shape[0], nb, kv_win * H, hd)

    qc = (np.arange(ATTN_SUB)[:, None] + ATTN_WINDOW) // CHUNK
    kc = np.arange(ATTN_KW)[None, :] // CHUNK
    band_ok = jnp.asarray((kc <= qc) & (kc >= qc - ATTN_WINDOW // CHUNK))

    st_p, st_s, kp_rows, vp_rows, ks_rows, vs_rows = [], [], [], [], [], []
    for layer in range(depth):
        j = layer // N_MIXERS
        is_hgrn = layer % N_MIXERS == 0
        proj_dtype = F32 if is_hgrn else BF16
        proj_p, proj_s = _in_proj(hp16, hs16, w_in, layer, tm_p, 1024, proj_dtype)
        proj_p = proj_p.reshape(B, S, 4 * D)
        proj_s = proj_s.reshape(nb, T, 4 * D)
        if is_hgrn:
            mix_p, s_fin = _hgrn_mixer(proj_p, hgrn_lb, hgrn_norm_g[j], j, min(HGRN_CHUNK, S),
                                       HGRN_HEADS_PROMPT)
            mix_s, s_new = _hgrn_mixer(proj_s, hgrn_lb, hgrn_norm_g[j], j, T, HGRN_HEADS_SAMPLE,
                                       s0=state_hgrn[j])
            st_p.append(s_fin.astype(state_hgrn.dtype))
            st_s.append(s_new.astype(state_hgrn.dtype))
        else:
            toep2 = _toeplitz_bias(attn_rel_bias[j].astype(F32), ATTN_SUB, ATTN_KW) * LOG2E
            mix_p = _attn_prompt(proj_p, jnp.where(band_ok, toep2, NEG_BIG))
            mix_s = _attn_sample(proj_s, cache_k, cache_v, j,
                                 toep2[:, :T, :kv_win], toep2[:, :T, kv_win:kv_win + T])
            out_dt = x_prompt.dtype
            kp_rows.append(proj_p[:, S - prompt_win:, D:2 * D].astype(out_dt).reshape(B, prompt_win, H, hd))
            vp_rows.append(proj_p[:, S - prompt_win:, 2 * D:3 * D].astype(out_dt).reshape(B, prompt_win, H, hd))
            ks_rows.append(proj_s[:, :, D:2 * D].astype(out_dt).reshape(nb, T, H, hd))
            vs_rows.append(proj_s[:, :, 2 * D:3 * D].astype(out_dt).reshape(nb, T, H, hd))
        hp, hp16 = _out_ln(mix_p.reshape(B * S, D), w_out16, layer, hp, ln_g[layer], ln_b[layer], alpha,
                           _row_tile(B * S, OUT_TM))
        hs, hs16 = _out_ln(mix_s.reshape(nb * T, D), w_out16, layer, hs, ln_g[layer], ln_b[layer], alpha,
                           _row_tile(nb * T, OUT_TM))
    return (hp.reshape(B, S, D), hs.reshape(nb, T, D),
            jnp.stack(st_p), jnp.stack(st_s),
            jnp.stack(kp_rows), jnp.stack(vp_rows), jnp.stack(ks_rows), jnp.stack(vs_rows))
```

```python
import functools

import numpy as np
import jax
import jax.numpy as jnp
from jax import lax
from jax.experimental import pallas as pl
from jax.experimental.pallas import tpu as pltpu

F32 = jnp.float32
BF16 = jnp.bfloat16

N_MIXERS = 2
HEADS = 16
CHUNK = 64
ATTN_WINDOW = 8 * CHUNK
MAX_REL = 256
LN_EPS = 1e-5
RMS_EPS = 1e-6
LB_TINY = 1e-30
NEG_BIG = -1e30

LANES = 128
SUBLANES = 8
BF16_ROWS = 16
HGRN_CHUNK = 128
HGRN_HEADS_PROMPT = 16
HGRN_HEADS_SAMPLE = 16
ATTN_TQ = ATTN_WINDOW
ATTN_SUB = 128
ATTN_KW = ATTN_SUB + ATTN_WINDOW
ATTN_HEADS = 8
GATE_SUB = 256
OUT_TM = 512
OUT_SUB = 128
VMEM_LIMIT = 48 * 1024 * 1024
LOG2E = 1.4426950408889634

_DIMS_NT = (((1,), (1,)), ((), ()))


def _silu(x):
    h = 0.5 * x
    return h * jnp.tanh(h) + h


def _dot_nt(x, y):
    return lax.dot_general(x, y, _DIMS_NT, preferred_element_type=F32)


def _forget_lower_bound(lb_ref, layer_j):
    lbp = lb_ref[...].astype(F32)
    pe = jnp.exp(lbp - jnp.max(lbp, axis=0, keepdims=True))
    prob = pe / jnp.sum(pe, axis=0, keepdims=True)
    cs = prob[0:1]
    for i in range(1, layer_j + 1):
        cs = cs + prob[i:i + 1]
    return cs - prob[0:1]


def _mm_kernel(*refs, mode, layer_j):
    if mode == "gate":
        x_ref, w_ref, lb_ref, a_ref, k_ref, w16_ref = refs
    else:
        x_ref, w_ref, o_ref, w16_ref = refs

    @pl.when(pl.program_id(1) == 0)
    def _():
        w16_ref[...] = w_ref[...].astype(BF16)

    if mode == "plain":
        o_ref[...] = jnp.dot(x_ref[...], w16_ref[...], preferred_element_type=F32).astype(o_ref.dtype)
    elif mode == "silu":
        o_ref[...] = _silu(jnp.dot(x_ref[...], w16_ref[...], preferred_element_type=F32)).astype(o_ref.dtype)
    else:
        lb = _forget_lower_bound(lb_ref, layer_j)
        lb_c = jnp.maximum(lb, LB_TINY)
        one_m_lb = 1.0 - lb
        sub = min(GATE_SUB, x_ref.shape[0])
        for r0 in range(0, x_ref.shape[0], sub):
            rows = slice(r0, r0 + sub)
            z = jnp.dot(x_ref[rows], w16_ref[...], preferred_element_type=F32)
            e = jnp.exp(-jnp.abs(z))
            pos = z >= 0
            inv = 1.0 / (1.0 + e)
            a_ref[rows] = jnp.log(jnp.where(pos, 1.0 + lb_c * e, e + lb_c) * inv)
            k_ref[rows] = one_m_lb * (jnp.where(pos, e, 1.0) * inv)


def _in_proj(x16, w, layer, col0, n_cols, tm, tn, mode="plain", out_dtype=F32, lb=None, layer_j=0):
    M, K = x16.shape
    assert M % tm == 0 and n_cols % tn == 0 and col0 % tn == 0
    cb0 = col0 // tn
    out_blk = pl.BlockSpec((tm, tn), lambda j, i: (i, j))
    w_mode = dict(pipeline_mode=pl.Buffered(1)) if mode == "gate" else {}
    in_specs = [pl.BlockSpec((tm, K), lambda j, i: (i, 0)),
                pl.BlockSpec((None, K, tn), lambda j, i: (layer, 0, cb0 + j), **w_mode)]
    args = [x16, w]
    if mode == "gate":
        in_specs.append(pl.BlockSpec((lb.shape[0], tn), lambda j, i: (0, j)))
        args.append(lb)
        out_shape = (jax.ShapeDtypeStruct((M, n_cols), F32), jax.ShapeDtypeStruct((M, n_cols), F32))
        out_specs = (out_blk, out_blk)
    else:
        out_shape = jax.ShapeDtypeStruct((M, n_cols), out_dtype)
        out_specs = out_blk
    return pl.pallas_call(
        functools.partial(_mm_kernel, mode=mode, layer_j=layer_j),
        out_shape=out_shape,
        grid=(n_cols // tn, M // tm),
        in_specs=in_specs,
        out_specs=out_specs,
        scratch_shapes=[pltpu.VMEM((K, tn), BF16)],
        compiler_params=pltpu.CompilerParams(
            dimension_semantics=("arbitrary", "arbitrary"), vmem_limit_bytes=VMEM_LIMIT),
        name="in_proj_" + mode,
    )(*args)


def _out_ln_kernel(m_ref, w_ref, h_ref, g_ref, b_ref, o_ref, o16_ref, *, alpha, sub):
    for r0 in range(0, m_ref.shape[0], sub):
        rows = slice(r0, r0 + sub)
        y = alpha * h_ref[rows] + jnp.dot(m_ref[rows], w_ref[...], preferred_element_type=F32)
        mu = jnp.mean(y, axis=-1, keepdims=True)
        yc = y - mu
        var = jnp.mean(yc * yc, axis=-1, keepdims=True)
        out = yc * lax.rsqrt(var + LN_EPS) * g_ref[...] + b_ref[...]
        o_ref[rows] = out
        o16_ref[rows] = out.astype(BF16)


def _out_ln(mix16, w16, layer, h, g, b, alpha, tm):
    M, D = h.shape
    assert M % tm == 0
    row = lambda i: (i, 0)
    const = lambda i: (0, 0)
    return pl.pallas_call(
        functools.partial(_out_ln_kernel, alpha=alpha, sub=min(tm, OUT_SUB)),
        out_shape=(jax.ShapeDtypeStruct((M, D), F32), jax.ShapeDtypeStruct((M, D), BF16)),
        grid=(M // tm,),
        in_specs=[pl.BlockSpec((tm, D), row),
                  pl.BlockSpec((None, D, D), lambda i: (layer, 0, 0), pipeline_mode=pl.Buffered(1)),
                  pl.BlockSpec((tm, D), row), pl.BlockSpec((1, D), const), pl.BlockSpec((1, D), const)],
        out_specs=(pl.BlockSpec((tm, D), row), pl.BlockSpec((tm, D), row)),
        compiler_params=pltpu.CompilerParams(
            dimension_semantics=("arbitrary",), vmem_limit_bytes=VMEM_LIMIT),
        name="out_proj_ln",
    )(mix16, w16, h, g.reshape(1, D), b.reshape(1, D))


def _hgrn_heads(qs, a, kk, v16, gs, ng, st_ref, mix_ref, masks, sub, C, heads):
    W = heads * LANES
    head = lambda x, h: x[:, h * LANES:(h + 1) * LANES]
    pair = lambda x, y: [_dot_nt(head(x, h), head(y, h)) for h in range(heads)]
    qs16 = qs.astype(BF16)
    kk16 = kk.astype(BF16)

    V = C // SUBLANES
    shape3 = (V, SUBLANES, W)
    p3 = a.reshape(shape3)
    qs3 = qs.reshape(shape3)
    kk3 = kk.reshape(shape3)
    parts = [pair(qs16, kk16)]
    for n in (1, 2, 4):
        if n == 1:
            bn = jnp.where((sub & 1) != 0, pltpu.roll(p3, 1, 1), p3)
        elif n == 2:
            bn = jnp.where(sub < 4, jnp.broadcast_to(p3[:, 1:2], shape3), jnp.broadcast_to(p3[:, 5:6], shape3))
        else:
            bn = jnp.broadcast_to(p3[:, 3:4], shape3)
        right = (sub & n) != 0
        ex = jnp.exp(jnp.where(right, p3, bn - p3))
        qn = (qs3 * ex).reshape(C, W).astype(BF16)
        kn = (kk3 * ex).reshape(C, W).astype(BF16)
        parts.append(pair(qn, kn))
        p3 = p3 + jnp.where(right, bn, 0.0)
    p = p3.reshape(C, W)

    n = SUBLANES
    while n < C:
        q_rows, k_rows, p_rows = [], [], []
        for lo in range(0, C, 2 * n):
            mid, hi = lo + n, lo + 2 * n
            bn = p[mid - 1:mid]
            k_left = kk[lo:mid] * jnp.exp(bn - p[lo:mid])
            q_right = qs[mid:hi] * jnp.exp(p[mid:hi])
            if n >= BF16_ROWS:
                q_rows += [qs16[lo:mid], q_right.astype(BF16)]
                k_rows += [k_left.astype(BF16), kk16[mid:hi]]
            else:
                q_rows += [qs[lo:mid], q_right]
                k_rows += [k_left, kk[mid:hi]]
            p_rows += [p[lo:mid], p[mid:hi] + bn]
        parts.append(pair(jnp.concatenate(q_rows, axis=0).astype(BF16),
                          jnp.concatenate(k_rows, axis=0).astype(BF16)))
        p = jnp.concatenate(p_rows, axis=0)
        n *= 2

    b_end = p[C - 1:C, :]
    qe = (qs * jnp.exp(p)).astype(BF16)
    kd = (kk * jnp.exp(b_end - p)).astype(BF16)
    dec = jnp.exp(b_end)
    scale = ng * gs

    for h in range(heads):
        amat = jnp.where(masks[0], parts[0][h], 0.0)
        for k in (1, 2, 3):
            amat = jnp.where(masks[k], parts[k][h], amat)
        n, k = SUBLANES, 4
        while n < C:
            rows = []
            for lo in range(0, C, 2 * n):
                mid, hi = lo + n, lo + 2 * n
                rows.append(amat[lo:mid])
                rows.append(jnp.where(masks[k][mid:hi], parts[k][h][mid:hi], amat[mid:hi]))
            amat = jnp.concatenate(rows, axis=0)
            n, k = 2 * n, k + 1

        st = st_ref[h]
        vh = head(v16, h)
        o = jnp.dot(amat.astype(BF16), vh, preferred_element_type=F32) + _dot_nt(head(qe, h), st.astype(BF16))
        st_ref[h] = st * head(dec, h) + jnp.dot(vh.T, head(kd, h), preferred_element_type=F32)
        ms = jnp.mean(o * o, axis=-1, keepdims=True)
        mix_ref[0, :, h * LANES:(h + 1) * LANES] = (o * lax.rsqrt(ms + RMS_EPS) * head(scale, h)).astype(mix_ref.dtype)


def _hgrn_kernel(*refs, chunk, heads, has_s0):
    if has_s0:
        lvl_ref, ng_ref, q_ref, a_ref, k_ref, v_ref, g_ref, s0_ref, mix_ref, sfin_ref, st_ref = refs
    else:
        lvl_ref, ng_ref, q_ref, a_ref, k_ref, v_ref, g_ref, mix_ref, sfin_ref, st_ref = refs
    C = chunk
    c = pl.program_id(2)

    @pl.when(c == 0)
    def _():
        for h in range(heads):
            if has_s0:
                st_ref[h] = s0_ref[0, h].T
            else:
                st_ref[h] = jnp.zeros((LANES, LANES), F32)

    ng = ng_ref[...].astype(F32)
    lvl = lvl_ref[...]
    n_levels = C.bit_length()
    masks = [lvl == k for k in range(n_levels)]
    sub = lax.broadcasted_iota(jnp.int32, (1, SUBLANES, heads * LANES), 1)

    _hgrn_heads(q_ref[0], a_ref[0], k_ref[0], v_ref[0], g_ref[0], ng, st_ref, mix_ref, masks, sub, C, heads)

    @pl.when(c == pl.num_programs(2) - 1)
    def _():
        for h in range(heads):
            sfin_ref[0, h] = st_ref[h].T


def _level_ids(C):
    t = np.arange(C)[:, None]
    s = np.arange(C)[None, :]
    x = t ^ s
    lv = np.full((C, C), -1, np.int32)
    lv[x == 0] = 0
    n, k = 1, 1
    while n < C:
        lv[(x >= n) & (x < 2 * n) & (t > s)] = k
        n, k = 2 * n, k + 1
    return jnp.asarray(lv)


def _hgrn_mixer(qs, a, kk, v16, gs, norm_g, chunk, heads, s0=None):
    nb, T, D = qs.shape
    H = HEADS
    assert D // H == LANES and T % chunk == 0 and H % heads == 0 and chunk % BF16_ROWS == 0
    has_s0 = s0 is not None
    hg = H // heads
    w = heads * LANES
    tok_blk = pl.BlockSpec((1, chunk, w), lambda b, h, c: (b, c, h))
    st_blk = pl.BlockSpec((1, heads, LANES, LANES), lambda b, h, c: (b, h, 0, 0))
    in_specs = [
        pl.BlockSpec((chunk, chunk), lambda b, h, c: (0, 0)),
        pl.BlockSpec((1, w), lambda b, h, c: (0, h)),
        tok_blk, tok_blk, tok_blk, tok_blk, tok_blk,
    ]
    args = [_level_ids(chunk), norm_g.reshape(1, D), qs, a, kk, v16, gs]
    if has_s0:
        in_specs.append(st_blk)
        args.append(s0)
    return pl.pallas_call(
        functools.partial(_hgrn_kernel, chunk=chunk, heads=heads, has_s0=has_s0),
        out_shape=(jax.ShapeDtypeStruct((nb, T, D), BF16),
                   jax.ShapeDtypeStruct((nb, H, LANES, LANES), F32)),
        grid=(nb, hg, T // chunk),
        in_specs=in_specs,
        out_specs=(pl.BlockSpec((1, chunk, w), lambda b, h, c: (b, c, h)), st_blk),
        scratch_shapes=[pltpu.VMEM((heads, LANES, LANES), F32)],
        compiler_params=pltpu.CompilerParams(
            dimension_semantics=("arbitrary", "arbitrary", "arbitrary"), vmem_limit_bytes=VMEM_LIMIT),
        name="hgrn_sample" if has_s0 else "hgrn_prompt",
    )(*args)


def _toeplitz_bias(table, rows, cols):
    H, n_rel = table.shape
    assert n_rel == MAX_REL + CHUNK
    d_hi = ATTN_WINDOW + rows - 1
    d_lo = ATTN_WINDOW - (cols - 1)
    left = d_hi - MAX_REL
    right = -(CHUNK - 1) - d_lo
    assert left >= 0 and right >= 0
    g = jnp.pad(table[:, ::-1], ((0, 0), (left, right)), mode="edge")
    n = rows + cols - 1
    g = jnp.pad(g, ((0, 0), (0, 1)))
    flat = jnp.tile(g, (1, rows))[:, rows - 1: rows - 1 + rows * n]
    return flat.reshape(H, rows, n)[:, :, :cols]


def _softmax2_pv(scores, values):
    m = functools.reduce(jnp.maximum, [jnp.max(s, axis=-1, keepdims=True) for s in scores])
    ps = [jnp.exp2(s - m) for s in scores]
    den = functools.reduce(lambda x, y: x + y, [jnp.sum(p, axis=-1, keepdims=True) for p in ps])
    o = functools.reduce(lambda x, y: x + y,
                         [jnp.dot(p.astype(BF16), v, preferred_element_type=F32) for p, v in zip(ps, values)])
    return o * (1.0 / den)


def _attn_prompt_kernel(bias_ref, q_ref, k_ref, v_ref, g_ref, mix_ref, kprev_ref, vprev_ref, *, qscale, heads):
    i = pl.program_id(2)
    tq = q_ref.shape[1]

    @pl.when(i == 0)
    def _():
        kprev_ref[...] = jnp.zeros_like(kprev_ref)
        vprev_ref[...] = jnp.zeros_like(vprev_ref)

    has_prev = i > 0
    for h in range(heads):
        sl = slice(h * LANES, (h + 1) * LANES)
        bias = bias_ref[h]
        q16 = (q_ref[0, :, sl].astype(F32) * qscale).astype(BF16)
        kcat = jnp.concatenate([kprev_ref[:, sl], k_ref[0, :, sl].astype(BF16)], axis=0)
        vcat = jnp.concatenate([vprev_ref[:, sl], v_ref[0, :, sl].astype(BF16)], axis=0)
        s_all = _dot_nt(q16, kcat)
        p_rows, inv_rows = [], []
        for r0 in range(0, tq, ATTN_SUB):
            n_prev = ATTN_WINDOW - r0
            s = s_all[r0:r0 + ATTN_SUB, r0:r0 + ATTN_KW] + bias
            s = jnp.concatenate([jnp.where(has_prev, s[:, :n_prev], NEG_BIG), s[:, n_prev:]], axis=1)
            p = jnp.exp2(s - jnp.max(s, axis=-1, keepdims=True))
            inv_rows.append(1.0 / jnp.sum(p, axis=-1, keepdims=True))
            pieces = [p.astype(BF16)]
            if r0:
                pieces.insert(0, jnp.zeros((ATTN_SUB, r0), BF16))
            if 2 * tq - ATTN_KW - r0:
                pieces.append(jnp.zeros((ATTN_SUB, 2 * tq - ATTN_KW - r0), BF16))
            p_rows.append(jnp.concatenate(pieces, axis=1))
        o = jnp.dot(jnp.concatenate(p_rows, axis=0), vcat, preferred_element_type=F32)
        o = o * jnp.concatenate(inv_rows, axis=0)
        mix_ref[0, :, sl] = (o * _silu(g_ref[0, :, sl].astype(F32))).astype(mix_ref.dtype)

    kprev_ref[...] = k_ref[0].astype(BF16)
    vprev_ref[...] = v_ref[0].astype(BF16)


def _attn_prompt(proj, bias2):
    B, S, D4 = proj.shape
    D = D4 // 4
    H = HEADS
    tq = ATTN_TQ
    heads = ATTN_HEADS
    hg = H // heads
    w = heads * LANES
    assert D // H == LANES and S % tq == 0 and H % heads == 0
    col = lambda off: (lambda b, h, i: (b, i, off * hg + h))
    blk = (1, tq, w)
    return pl.pallas_call(
        functools.partial(_attn_prompt_kernel, qscale=float(LANES) ** -0.5 * LOG2E, heads=heads),
        out_shape=jax.ShapeDtypeStruct((B, S, D), BF16),
        grid=(B, hg, S // tq),
        in_specs=[pl.BlockSpec((heads, ATTN_SUB, ATTN_KW), lambda b, h, i: (h, 0, 0)),
                  pl.BlockSpec(blk, col(0)), pl.BlockSpec(blk, col(1)),
                  pl.BlockSpec(blk, col(2)), pl.BlockSpec(blk, col(3))],
        out_specs=pl.BlockSpec(blk, lambda b, h, i: (b, i, h)),
        scratch_shapes=[pltpu.VMEM((tq, w), BF16), pltpu.VMEM((tq, w), BF16)],
        compiler_params=pltpu.CompilerParams(
            dimension_semantics=("arbitrary", "arbitrary", "arbitrary"), vmem_limit_bytes=VMEM_LIMIT),
        name="attn_prompt",
    )(bias2, proj, proj, proj, proj)


def _attn_sample_kernel(bc_ref, bn_ref, q_ref, kn_ref, vn_ref, g_ref, kc_ref, vc_ref, mix_ref, *, qscale, heads):
    win = kc_ref.shape[1] // heads
    for h in range(heads):
        sl = slice(h * LANES, (h + 1) * LANES)
        kc = kc_ref[0, pl.ds(h, win, stride=heads), :].astype(BF16)
        vc = vc_ref[0, pl.ds(h, win, stride=heads), :].astype(BF16)
        q16 = (q_ref[0, :, sl].astype(F32) * qscale).astype(BF16)
        s_c = _dot_nt(q16, kc) + bc_ref[h]
        s_n = _dot_nt(q16, kn_ref[0, :, sl].astype(BF16)) + bn_ref[h]
        o = _softmax2_pv([s_c, s_n], [vc, vn_ref[0, :, sl].astype(BF16)])
        mix_ref[0, :, sl] = (o * _silu(g_ref[0, :, sl].astype(F32))).astype(mix_ref.dtype)


def _attn_sample(proj, cache_k, cache_v, layer_j, bias_c2, bias_n2):
    nb, T, D4 = proj.shape
    D = D4 // 4
    H = HEADS
    W = cache_k.shape[2] // H
    new = lambda off: (lambda b: (b, 0, off))
    blk = (1, T, D)
    cblk = (None, 1, W * H, LANES)
    return pl.pallas_call(
        functools.partial(_attn_sample_kernel, qscale=float(LANES) ** -0.5 * LOG2E, heads=H),
        out_shape=jax.ShapeDtypeStruct((nb, T, D), BF16),
        grid=(nb,),
        in_specs=[pl.BlockSpec((H, T, W), lambda b: (0, 0, 0)),
                  pl.BlockSpec((H, T, T), lambda b: (0, 0, 0)),
                  pl.BlockSpec(blk, new(0)), pl.BlockSpec(blk, new(1)),
                  pl.BlockSpec(blk, new(2)), pl.BlockSpec(blk, new(3)),
                  pl.BlockSpec(cblk, lambda b: (layer_j, b, 0, 0)),
                  pl.BlockSpec(cblk, lambda b: (layer_j, b, 0, 0))],
        out_specs=pl.BlockSpec(blk, lambda b: (b, 0, 0)),
        compiler_params=pltpu.CompilerParams(
            dimension_semantics=("arbitrary",), vmem_limit_bytes=VMEM_LIMIT),
        name="attn_sample",
    )(bias_c2, bias_n2, proj, proj, proj, proj, cache_k, cache_v)


def _row_tile(m, cap):
    t = cap
    while m % t:
        t //= 2
    return t


def kernel(x_prompt, x_sample, state_hgrn, cache_attn_k, cache_attn_v, w_in, w_out, ln_g, ln_b,
           hgrn_lb, hgrn_norm_g, attn_rel_bias):
    B, S, D = x_prompt.shape
    nb, T, _ = x_sample.shape
    depth = w_in.shape[0]
    H = HEADS
    hd = D // H
    alpha = (2 * depth) ** 0.25
    kv_win = cache_attn_k.shape[2]
    prompt_win = min(ATTN_WINDOW, S)
    assert kv_win == ATTN_WINDOW

    w_out16 = w_out.astype(BF16)
    hp = x_prompt.reshape(B * S, D)
    hs = x_sample.reshape(nb * T, D)
    hp16 = hp.astype(BF16)
    hs16 = hs.astype(BF16)
    tm_p = _row_tile(B * S, 1024)
    tm_s = _row_tile(nb * T, 512)
    tn = 1024
    cache_k = cache_attn_k.reshape(cache_attn_k.shape[0], nb, kv_win * H, hd)
    cache_v = cache_attn_v.reshape(cache_attn_v.shape[0], nb, kv_win * H, hd)

    qc = (np.arange(ATTN_SUB)[:, None] + ATTN_WINDOW) // CHUNK
    kc = np.arange(ATTN_KW)[None, :] // CHUNK
    band_ok = jnp.asarray((kc <= qc) & (kc >= qc - ATTN_WINDOW // CHUNK))

    st_p, st_s, kp_rows, vp_rows, ks_rows, vs_rows = [], [], [], [], [], []
    for layer in range(depth):
        j = layer // N_MIXERS
        if layer % N_MIXERS == 0:
            mixes = []
            for x16, nseq, tm, chunk, heads, s0 in (
                    (hp16, B, tm_p, min(HGRN_CHUNK, S), HGRN_HEADS_PROMPT, None),
                    (hs16, nb, tm_s, T, HGRN_HEADS_SAMPLE, state_hgrn[j])):
                sec = lambda y: y.reshape(nseq, -1, D)
                qs = _in_proj(x16, w_in, layer, 0, D, tm, tn, "silu")
                a, kk = _in_proj(x16, w_in, layer, D, D, tm, tn, "gate", lb=hgrn_lb, layer_j=j)
                v16 = _in_proj(x16, w_in, layer, 2 * D, D, tm, tn, "plain", BF16)
                gs = _in_proj(x16, w_in, layer, 3 * D, D, tm, tn, "silu")
                mixes.append(_hgrn_mixer(sec(qs), sec(a), sec(kk), sec(v16), sec(gs), hgrn_norm_g[j],
                                         chunk, heads, s0=s0))
            (mix_p, s_fin), (mix_s, s_new) = mixes
            st_p.append(s_fin.astype(state_hgrn.dtype))
            st_s.append(s_new.astype(state_hgrn.dtype))
        else:
            proj_p = _in_proj(hp16, w_in, layer, 0, 4 * D, tm_p, tn, "plain", BF16).reshape(B, S, 4 * D)
            proj_s = _in_proj(hs16, w_in, layer, 0, 4 * D, tm_s, tn, "plain", BF16).reshape(nb, T, 4 * D)
            toep2 = _toeplitz_bias(attn_rel_bias[j].astype(F32), ATTN_SUB, ATTN_KW) * LOG2E
            mix_p = _attn_prompt(proj_p, jnp.where(band_ok, toep2, NEG_BIG))
            mix_s = _attn_sample(proj_s, cache_k, cache_v, j,
                                 toep2[:, :T, :kv_win], toep2[:, :T, kv_win:kv_win + T])
            out_dt = x_prompt.dtype
            kp_rows.append(proj_p[:, S - prompt_win:, D:2 * D].astype(out_dt).reshape(B, prompt_win, H, hd))
            vp_rows.append(proj_p[:, S - prompt_win:, 2 * D:3 * D].astype(out_dt).reshape(B, prompt_win, H, hd))
            ks_rows.append(proj_s[:, :, D:2 * D].astype(out_dt).reshape(nb, T, H, hd))
            vs_rows.append(proj_s[:, :, 2 * D:3 * D].astype(out_dt).reshape(nb, T, H, hd))
        hp, hp16 = _out_ln(mix_p.reshape(B * S, D), w_out16, layer, hp, ln_g[layer], ln_b[layer], alpha,
                           _row_tile(B * S, OUT_TM))
        hs, hs16 = _out_ln(mix_s.reshape(nb * T, D), w_out16, layer, hs, ln_g[layer], ln_b[layer], alpha,
                           _row_tile(nb * T, OUT_TM))
    return (hp.reshape(B, S, D), hs.reshape(nb, T, D),
            jnp.stack(st_p), jnp.stack(st_s),
            jnp.stack(kp_rows), jnp.stack(vp_rows), jnp.stack(ks_rows), jnp.stack(vs_rows))
```

```python
import functools

import numpy as np
import jax
import jax.numpy as jnp
from jax import lax
from jax.experimental import pallas as pl
from jax.experimental.pallas import tpu as pltpu

F32 = jnp.float32
BF16 = jnp.bfloat16

N_MIXERS = 2
HEADS = 16
CHUNK = 64
ATTN_WINDOW = 8 * CHUNK
MAX_REL = 256
LN_EPS = 1e-5
RMS_EPS = 1e-6
LB_TINY = 1e-30
NEG_BIG = -1e30

LANES = 128
SUBLANES = 8
BF16_ROWS = 16
HGRN_CHUNK = 128
HGRN_HEADS_PROMPT = 16
HGRN_HEADS_SAMPLE = 16
ATTN_TQ = ATTN_WINDOW
ATTN_SUB = 128
ATTN_KW = ATTN_SUB + ATTN_WINDOW
ATTN_HEADS = 8
GATE_SUB = 128
OUT_TM = 512
OUT_SUB = 128
VMEM_LIMIT = 48 * 1024 * 1024
VMEM_LIMIT_BIG = 58 * 1024 * 1024
LOG2E = 1.4426950408889634

_DIMS_NT = (((1,), (1,)), ((), ()))


def _silu(x):
    h = 0.5 * x
    return h * jnp.tanh(h) + h


def _dot_nt(x, y):
    return lax.dot_general(x, y, _DIMS_NT, preferred_element_type=F32)


def _forget_lower_bound(lb_ref, layer_j):
    lbp = lb_ref[...].astype(F32)
    pe = jnp.exp(lbp - jnp.max(lbp, axis=0, keepdims=True))
    prob = pe / jnp.sum(pe, axis=0, keepdims=True)
    cs = prob[0:1]
    for i in range(1, layer_j + 1):
        cs = cs + prob[i:i + 1]
    return cs - prob[0:1]


def _mm_kernel(*refs, mode, layer_j):
    if mode == "gate":
        x_ref, w_ref, lb_ref, a_ref, k_ref, w16_ref = refs
    else:
        x_ref, w_ref, o_ref, w16_ref = refs

    @pl.when(pl.program_id(1) == 0)
    def _():
        w16_ref[...] = w_ref[...].astype(BF16)

    if mode == "plain":
        o_ref[...] = jnp.dot(x_ref[...], w16_ref[...], preferred_element_type=F32).astype(o_ref.dtype)
    elif mode == "silu":
        o_ref[...] = _silu(jnp.dot(x_ref[...], w16_ref[...], preferred_element_type=F32)).astype(o_ref.dtype)
    else:
        lb = _forget_lower_bound(lb_ref, layer_j)
        lb_c = jnp.maximum(lb, LB_TINY)
        one_m_lb = 1.0 - lb
        def gate(rows):
            z = a_ref[rows]
            e = jnp.exp(-jnp.abs(z))
            pos = z >= 0
            inv = 1.0 / (1.0 + e)
            a_ref[rows] = jnp.log(jnp.where(pos, 1.0 + lb_c * e, e + lb_c) * inv)
            k_ref[rows] = one_m_lb * (jnp.where(pos, e, 1.0) * inv)

        sub = min(GATE_SUB, x_ref.shape[0])
        prev = None
        for r0 in range(0, x_ref.shape[0], sub):
            rows = slice(r0, r0 + sub)
            a_ref[rows] = jnp.dot(x_ref[rows], w16_ref[...], preferred_element_type=F32)
            if prev is not None:
                gate(prev)
            prev = rows
        gate(prev)


def _in_proj(x16, w, layer, col0, n_cols, tm, tn, mode="plain", out_dtype=F32, lb=None, layer_j=0):
    M, K = x16.shape
    assert M % tm == 0 and n_cols % tn == 0 and col0 % tn == 0
    cb0 = col0 // tn
    out_blk = pl.BlockSpec((tm, tn), lambda j, i: (i, j))
    w_mode = dict(pipeline_mode=pl.Buffered(1)) if mode == "gate" else {}
    in_specs = [pl.BlockSpec((tm, K), lambda j, i: (i, 0)),
                pl.BlockSpec((None, K, tn), lambda j, i: (layer, 0, cb0 + j), **w_mode)]
    args = [x16, w]
    if mode == "gate":
        in_specs.append(pl.BlockSpec((lb.shape[0], tn), lambda j, i: (0, j)))
        args.append(lb)
        out_shape = (jax.ShapeDtypeStruct((M, n_cols), F32), jax.ShapeDtypeStruct((M, n_cols), F32))
        out_specs = (out_blk, out_blk)
    else:
        out_shape = jax.ShapeDtypeStruct((M, n_cols), out_dtype)
        out_specs = out_blk
    return pl.pallas_call(
        functools.partial(_mm_kernel, mode=mode, layer_j=layer_j),
        out_shape=out_shape,
        grid=(n_cols // tn, M // tm),
        in_specs=in_specs,
        out_specs=out_specs,
        scratch_shapes=[pltpu.VMEM((K, tn), BF16)],
        compiler_params=pltpu.CompilerParams(
            dimension_semantics=("arbitrary", "arbitrary"),
            vmem_limit_bytes=VMEM_LIMIT_BIG if tm > 1024 else VMEM_LIMIT),
        name="in_proj_" + mode,
    )(*args)


def _out_ln_kernel(m_ref, w_ref, h_ref, g_ref, b_ref, o_ref, o16_ref, *, alpha, sub):
    for r0 in range(0, m_ref.shape[0], sub):
        rows = slice(r0, r0 + sub)
        y = alpha * h_ref[rows] + jnp.dot(m_ref[rows], w_ref[...], preferred_element_type=F32)
        mu = jnp.mean(y, axis=-1, keepdims=True)
        yc = y - mu
        var = jnp.mean(yc * yc, axis=-1, keepdims=True)
        out = yc * lax.rsqrt(var + LN_EPS) * g_ref[...] + b_ref[...]
        o_ref[rows] = out
        o16_ref[rows] = out.astype(BF16)


def _out_ln(mix16, w16, layer, h, g, b, alpha, tm):
    M, D = h.shape
    assert M % tm == 0
    row = lambda i: (i, 0)
    const = lambda i: (0, 0)
    return pl.pallas_call(
        functools.partial(_out_ln_kernel, alpha=alpha, sub=min(tm, OUT_SUB)),
        out_shape=(jax.ShapeDtypeStruct((M, D), F32), jax.ShapeDtypeStruct((M, D), BF16)),
        grid=(M // tm,),
        in_specs=[pl.BlockSpec((tm, D), row),
                  pl.BlockSpec((None, D, D), lambda i: (layer, 0, 0), pipeline_mode=pl.Buffered(1)),
                  pl.BlockSpec((tm, D), row), pl.BlockSpec((1, D), const), pl.BlockSpec((1, D), const)],
        out_specs=(pl.BlockSpec((tm, D), row), pl.BlockSpec((tm, D), row)),
        compiler_params=pltpu.CompilerParams(
            dimension_semantics=("arbitrary",), vmem_limit_bytes=VMEM_LIMIT),
        name="out_proj_ln",
    )(mix16, w16, h, g.reshape(1, D), b.reshape(1, D))


def _hgrn_heads(qs, a, kk, v16, gs, ng, st_ref, mix_ref, masks, sub, C, heads):
    W = heads * LANES
    head = lambda x, h: x[:, h * LANES:(h + 1) * LANES]
    pair = lambda x, y: [_dot_nt(head(x, h), head(y, h)) for h in range(heads)]
    qs16 = qs.astype(BF16)
    kk16 = kk.astype(BF16)

    V = C // SUBLANES
    shape3 = (V, SUBLANES, W)
    p3 = a.reshape(shape3)
    qs3 = qs.reshape(shape3)
    kk3 = kk.reshape(shape3)
    parts = [pair(qs16, kk16)]
    for n in (1, 2, 4):
        if n == 1:
            bn = jnp.where((sub & 1) != 0, pltpu.roll(p3, 1, 1), p3)
        elif n == 2:
            bn = jnp.where(sub < 4, jnp.broadcast_to(p3[:, 1:2], shape3), jnp.broadcast_to(p3[:, 5:6], shape3))
        else:
            bn = jnp.broadcast_to(p3[:, 3:4], shape3)
        right = (sub & n) != 0
        ex = jnp.exp(jnp.where(right, p3, bn - p3))
        qn = (qs3 * ex).reshape(C, W).astype(BF16)
        kn = (kk3 * ex).reshape(C, W).astype(BF16)
        parts.append(pair(qn, kn))
        p3 = p3 + jnp.where(right, bn, 0.0)
    p = p3.reshape(C, W)

    n = SUBLANES
    while n < C:
        q_rows, k_rows, p_rows = [], [], []
        for lo in range(0, C, 2 * n):
            mid, hi = lo + n, lo + 2 * n
            bn = p[mid - 1:mid]
            k_left = kk[lo:mid] * jnp.exp(bn - p[lo:mid])
            q_right = qs[mid:hi] * jnp.exp(p[mid:hi])
            if n >= BF16_ROWS:
                q_rows += [qs16[lo:mid], q_right.astype(BF16)]
                k_rows += [k_left.astype(BF16), kk16[mid:hi]]
            else:
                q_rows += [qs[lo:mid], q_right]
                k_rows += [k_left, kk[mid:hi]]
            p_rows += [p[lo:mid], p[mid:hi] + bn]
        parts.append(pair(jnp.concatenate(q_rows, axis=0).astype(BF16),
                          jnp.concatenate(k_rows, axis=0).astype(BF16)))
        p = jnp.concatenate(p_rows, axis=0)
        n *= 2

    b_end = p[C - 1:C, :]
    qe = (qs * jnp.exp(p)).astype(BF16)
    kd = (kk * jnp.exp(b_end - p)).astype(BF16)
    dec = jnp.exp(b_end)
    scale = ng * gs

    for h in range(heads):
        amat = jnp.where(masks[0], parts[0][h], 0.0)
        for k in (1, 2, 3):
            amat = jnp.where(masks[k], parts[k][h], amat)
        n, k = SUBLANES, 4
        while n < C:
            rows = []
            for lo in range(0, C, 2 * n):
                mid, hi = lo + n, lo + 2 * n
                rows.append(amat[lo:mid])
                rows.append(jnp.where(masks[k][mid:hi], parts[k][h][mid:hi], amat[mid:hi]))
            amat = jnp.concatenate(rows, axis=0)
            n, k = 2 * n, k + 1

        st = st_ref[h]
        vh = head(v16, h)
        o = jnp.dot(amat.astype(BF16), vh, preferred_element_type=F32) + _dot_nt(head(qe, h), st.astype(BF16))
        st_ref[h] = st * head(dec, h) + jnp.dot(vh.T, head(kd, h), preferred_element_type=F32)
        ms = jnp.mean(o * o, axis=-1, keepdims=True)
        mix_ref[0, :, h * LANES:(h + 1) * LANES] = (o * lax.rsqrt(ms + RMS_EPS) * head(scale, h)).astype(mix_ref.dtype)


def _hgrn_kernel(*refs, chunk, heads, has_s0):
    if has_s0:
        lvl_ref, ng_ref, q_ref, a_ref, k_ref, v_ref, g_ref, s0_ref, mix_ref, sfin_ref, st_ref = refs
    else:
        lvl_ref, ng_ref, q_ref, a_ref, k_ref, v_ref, g_ref, mix_ref, sfin_ref, st_ref = refs
    C = chunk
    c = pl.program_id(2)

    @pl.when(c == 0)
    def _():
        for h in range(heads):
            if has_s0:
                st_ref[h] = s0_ref[0, h].T
            else:
                st_ref[h] = jnp.zeros((LANES, LANES), F32)

    ng = ng_ref[...].astype(F32)
    lvl = lvl_ref[...]
    n_levels = C.bit_length()
    masks = [lvl == k for k in range(n_levels)]
    sub = lax.broadcasted_iota(jnp.int32, (1, SUBLANES, heads * LANES), 1)

    _hgrn_heads(q_ref[0], a_ref[0], k_ref[0], v_ref[0], g_ref[0], ng, st_ref, mix_ref, masks, sub, C, heads)

    @pl.when(c == pl.num_programs(2) - 1)
    def _():
        for h in range(heads):
            sfin_ref[0, h] = st_ref[h].T


def _level_ids(C):
    t = np.arange(C)[:, None]
    s = np.arange(C)[None, :]
    x = t ^ s
    lv = np.full((C, C), -1, np.int32)
    lv[x == 0] = 0
    n, k = 1, 1
    while n < C:
        lv[(x >= n) & (x < 2 * n) & (t > s)] = k
        n, k = 2 * n, k + 1
    return jnp.asarray(lv)


def _hgrn_mixer(qs, a, kk, v16, gs, norm_g, chunk, heads, s0=None):
    nb, T, D = qs.shape
    H = HEADS
    assert D // H == LANES and T % chunk == 0 and H % heads == 0 and chunk % BF16_ROWS == 0
    has_s0 = s0 is not None
    hg = H // heads
    w = heads * LANES
    tok_blk = pl.BlockSpec((1, chunk, w), lambda b, h, c: (b, c, h))
    st_blk = pl.BlockSpec((1, heads, LANES, LANES), lambda b, h, c: (b, h, 0, 0))
    in_specs = [
        pl.BlockSpec((chunk, chunk), lambda b, h, c: (0, 0)),
        pl.BlockSpec((1, w), lambda b, h, c: (0, h)),
        tok_blk, tok_blk, tok_blk, tok_blk, tok_blk,
    ]
    args = [_level_ids(chunk), norm_g.reshape(1, D), qs, a, kk, v16, gs]
    if has_s0:
        in_specs.append(st_blk)
        args.append(s0)
    return pl.pallas_call(
        functools.partial(_hgrn_kernel, chunk=chunk, heads=heads, has_s0=has_s0),
        out_shape=(jax.ShapeDtypeStruct((nb, T, D), BF16),
                   jax.ShapeDtypeStruct((nb, H, LANES, LANES), F32)),
        grid=(nb, hg, T // chunk),
        in_specs=in_specs,
        out_specs=(pl.BlockSpec((1, chunk, w), lambda b, h, c: (b, c, h)), st_blk),
        scratch_shapes=[pltpu.VMEM((heads, LANES, LANES), F32)],
        compiler_params=pltpu.CompilerParams(
            dimension_semantics=("arbitrary", "arbitrary", "arbitrary"), vmem_limit_bytes=VMEM_LIMIT),
        name="hgrn_sample" if has_s0 else "hgrn_prompt",
    )(*args)


def _toeplitz_bias(table, rows, cols):
    H, n_rel = table.shape
    assert n_rel == MAX_REL + CHUNK
    d_hi = ATTN_WINDOW + rows - 1
    d_lo = ATTN_WINDOW - (cols - 1)
    left = d_hi - MAX_REL
    right = -(CHUNK - 1) - d_lo
    assert left >= 0 and right >= 0
    g = jnp.pad(table[:, ::-1], ((0, 0), (left, right)), mode="edge")
    n = rows + cols - 1
    g = jnp.pad(g, ((0, 0), (0, 1)))
    flat = jnp.tile(g, (1, rows))[:, rows - 1: rows - 1 + rows * n]
    return flat.reshape(H, rows, n)[:, :, :cols]


def _softmax2_pv(scores, values):
    m = functools.reduce(jnp.maximum, [jnp.max(s, axis=-1, keepdims=True) for s in scores])
    ps = [jnp.exp2(s - m) for s in scores]
    den = functools.reduce(lambda x, y: x + y, [jnp.sum(p, axis=-1, keepdims=True) for p in ps])
    o = functools.reduce(lambda x, y: x + y,
                         [jnp.dot(p.astype(BF16), v, preferred_element_type=F32) for p, v in zip(ps, values)])
    return o * (1.0 / den)


def _attn_prompt_kernel(bias_ref, q_ref, k_ref, v_ref, g_ref, mix_ref, kprev_ref, vprev_ref, *, qscale, heads):
    i = pl.program_id(2)
    tq = q_ref.shape[1]

    @pl.when(i == 0)
    def _():
        kprev_ref[...] = jnp.zeros_like(kprev_ref)
        vprev_ref[...] = jnp.zeros_like(vprev_ref)

    has_prev = i > 0
    for h in range(heads):
        sl = slice(h * LANES, (h + 1) * LANES)
        bias = bias_ref[h]
        q16 = (q_ref[0, :, sl].astype(F32) * qscale).astype(BF16)
        kcat = jnp.concatenate([kprev_ref[:, sl], k_ref[0, :, sl].astype(BF16)], axis=0)
        vcat_t = jnp.concatenate([vprev_ref[:, sl], v_ref[0, :, sl].astype(BF16)], axis=0).T
        s_all = _dot_nt(kcat, q16)
        p_cols, inv_cols = [], []
        for r0 in range(0, tq, ATTN_SUB):
            n_prev = ATTN_WINDOW - r0
            s = s_all[r0:r0 + ATTN_KW, r0:r0 + ATTN_SUB] + bias
            s = jnp.concatenate([jnp.where(has_prev, s[:n_prev], NEG_BIG), s[n_prev:]], axis=0)
            p = jnp.exp2(s - jnp.max(s, axis=0, keepdims=True))
            inv_cols.append(1.0 / jnp.sum(p, axis=0, keepdims=True))
            pieces = [p.astype(BF16)]
            if r0:
                pieces.insert(0, jnp.zeros((r0, ATTN_SUB), BF16))
            if 2 * tq - ATTN_KW - r0:
                pieces.append(jnp.zeros((2 * tq - ATTN_KW - r0, ATTN_SUB), BF16))
            p_cols.append(jnp.concatenate(pieces, axis=0))
        o_t = jnp.dot(vcat_t, jnp.concatenate(p_cols, axis=1), preferred_element_type=F32)
        o = (o_t * jnp.concatenate(inv_cols, axis=1)).T
        mix_ref[0, :, sl] = (o * _silu(g_ref[0, :, sl].astype(F32))).astype(mix_ref.dtype)

    kprev_ref[...] = k_ref[0].astype(BF16)
    vprev_ref[...] = v_ref[0].astype(BF16)


def _attn_prompt(proj, bias2):
    B, S, D4 = proj.shape
    D = D4 // 4
    H = HEADS
    tq = ATTN_TQ
    heads = ATTN_HEADS
    hg = H // heads
    w = heads * LANES
    assert D // H == LANES and S % tq == 0 and H % heads == 0
    col = lambda off: (lambda b, h, i: (b, i, off * hg + h))
    blk = (1, tq, w)
    return pl.pallas_call(
        functools.partial(_attn_prompt_kernel, qscale=float(LANES) ** -0.5 * LOG2E, heads=heads),
        out_shape=jax.ShapeDtypeStruct((B, S, D), BF16),
        grid=(B, hg, S // tq),
        in_specs=[pl.BlockSpec((heads, ATTN_KW, ATTN_SUB), lambda b, h, i: (h, 0, 0)),
                  pl.BlockSpec(blk, col(0)), pl.BlockSpec(blk, col(1)),
                  pl.BlockSpec(blk, col(2)), pl.BlockSpec(blk, col(3))],
        out_specs=pl.BlockSpec(blk, lambda b, h, i: (b, i, h)),
        scratch_shapes=[pltpu.VMEM((tq, w), BF16), pltpu.VMEM((tq, w), BF16)],
        compiler_params=pltpu.CompilerParams(
            dimension_semantics=("arbitrary", "arbitrary", "arbitrary"), vmem_limit_bytes=VMEM_LIMIT),
        name="attn_prompt",
    )(bias2, proj, proj, proj, proj)


def _attn_sample_kernel(bc_ref, bn_ref, q_ref, kn_ref, vn_ref, g_ref, kc_ref, vc_ref, mix_ref, *, qscale, heads):
    win = kc_ref.shape[1] // heads
    for h in range(heads):
        sl = slice(h * LANES, (h + 1) * LANES)
        kc = kc_ref[0, pl.ds(h, win, stride=heads), :].astype(BF16)
        vc = vc_ref[0, pl.ds(h, win, stride=heads), :].astype(BF16)
        q16 = (q_ref[0, :, sl].astype(F32) * qscale).astype(BF16)
        s_c = _dot_nt(q16, kc) + bc_ref[h]
        s_n = _dot_nt(q16, kn_ref[0, :, sl].astype(BF16)) + bn_ref[h]
        o = _softmax2_pv([s_c, s_n], [vc, vn_ref[0, :, sl].astype(BF16)])
        mix_ref[0, :, sl] = (o * _silu(g_ref[0, :, sl].astype(F32))).astype(mix_ref.dtype)


def _attn_sample(proj, cache_k, cache_v, layer_j, bias_c2, bias_n2):
    nb, T, D4 = proj.shape
    D = D4 // 4
    H = HEADS
    W = cache_k.shape[2] // H
    new = lambda off: (lambda b: (b, 0, off))
    blk = (1, T, D)
    cblk = (None, 1, W * H, LANES)
    return pl.pallas_call(
        functools.partial(_attn_sample_kernel, qscale=float(LANES) ** -0.5 * LOG2E, heads=H),
        out_shape=jax.ShapeDtypeStruct((nb, T, D), BF16),
        grid=(nb,),
        in_specs=[pl.BlockSpec((H, T, W), lambda b: (0, 0, 0)),
                  pl.BlockSpec((H, T, T), lambda b: (0, 0, 0)),
                  pl.BlockSpec(blk, new(0)), pl.BlockSpec(blk, new(1)),
                  pl.BlockSpec(blk, new(2)), pl.BlockSpec(blk, new(3)),
                  pl.BlockSpec(cblk, lambda b: (layer_j, b, 0, 0)),
                  pl.BlockSpec(cblk, lambda b: (layer_j, b, 0, 0))],
        out_specs=pl.BlockSpec(blk, lambda b: (b, 0, 0)),
        compiler_params=pltpu.CompilerParams(
            dimension_semantics=("arbitrary",), vmem_limit_bytes=VMEM_LIMIT),
        name="attn_sample",
    )(bias_c2, bias_n2, proj, proj, proj, proj, cache_k, cache_v)


def _row_tile(m, cap):
    t = cap
    while m % t:
        t //= 2
    return t


def kernel(x_prompt, x_sample, state_hgrn, cache_attn_k, cache_attn_v, w_in, w_out, ln_g, ln_b,
           hgrn_lb, hgrn_norm_g, attn_rel_bias):
    B, S, D = x_prompt.shape
    nb, T, _ = x_sample.shape
    depth = w_in.shape[0]
    H = HEADS
    hd = D // H
    alpha = (2 * depth) ** 0.25
    kv_win = cache_attn_k.shape[2]
    prompt_win = min(ATTN_WINDOW, S)
    assert kv_win == ATTN_WINDOW

    w_out16 = w_out.astype(BF16)
    hp = x_prompt.reshape(B * S, D)
    hs = x_sample.reshape(nb * T, D)
    hp16 = hp.astype(BF16)
    hs16 = hs.astype(BF16)
    tm_p = _row_tile(B * S, 1024)
    tm_s = _row_tile(nb * T, 512)
    tn = 1024
    cache_k = cache_attn_k.reshape(cache_attn_k.shape[0], nb, kv_win * H, hd)
    cache_v = cache_attn_v.reshape(cache_attn_v.shape[0], nb, kv_win * H, hd)

    qc = (np.arange(ATTN_SUB)[:, None] + ATTN_WINDOW) // CHUNK
    kc = np.arange(ATTN_KW)[None, :] // CHUNK
    band_ok = jnp.asarray((kc <= qc) & (kc >= qc - ATTN_WINDOW // CHUNK))

    st_p, st_s, kp_rows, vp_rows, ks_rows, vs_rows = [], [], [], [], [], []
    for layer in range(depth):
        j = layer // N_MIXERS
        if layer % N_MIXERS == 0:
            mixes = []
            for x16, nseq, tm, chunk, heads, s0 in (
                    (hp16, B, tm_p, min(HGRN_CHUNK, S), HGRN_HEADS_PROMPT, None),
                    (hs16, nb, tm_s, T, HGRN_HEADS_SAMPLE, state_hgrn[j])):
                sec = lambda y: y.reshape(nseq, -1, D)
                qs = _in_proj(x16, w_in, layer, 0, D, tm, tn, "silu")
                a, kk = _in_proj(x16, w_in, layer, D, D, tm, tn, "gate", lb=hgrn_lb, layer_j=j)
                v16 = _in_proj(x16, w_in, layer, 2 * D, D, tm, tn, "plain", BF16)
                gs = _in_proj(x16, w_in, layer, 3 * D, D, tm, tn, "silu")
                mixes.append(_hgrn_mixer(sec(qs), sec(a), sec(kk), sec(v16), sec(gs), hgrn_norm_g[j],
                                         chunk, heads, s0=s0))
            (mix_p, s_fin), (mix_s, s_new) = mixes
            st_p.append(s_fin.astype(state_hgrn.dtype))
            st_s.append(s_new.astype(state_hgrn.dtype))
        else:
            proj_p = _in_proj(hp16, w_in, layer, 0, 4 * D, _row_tile(B * S, 2048), tn, "plain",
                              BF16).reshape(B, S, 4 * D)
            proj_s = _in_proj(hs16, w_in, layer, 0, 4 * D, tm_s, tn, "plain", BF16).reshape(nb, T, 4 * D)
            toep2 = _toeplitz_bias(attn_rel_bias[j].astype(F32), ATTN_SUB, ATTN_KW) * LOG2E
            mix_p = _attn_prompt(proj_p, jnp.swapaxes(jnp.where(band_ok, toep2, NEG_BIG), 1, 2))
            mix_s = _attn_sample(proj_s, cache_k, cache_v, j,
                                 toep2[:, :T, :kv_win], toep2[:, :T, kv_win:kv_win + T])
            out_dt = x_prompt.dtype
            kp_rows.append(proj_p[:, S - prompt_win:, D:2 * D].astype(out_dt).reshape(B, prompt_win, H, hd))
            vp_rows.append(proj_p[:, S - prompt_win:, 2 * D:3 * D].astype(out_dt).reshape(B, prompt_win, H, hd))
            ks_rows.append(proj_s[:, :, D:2 * D].astype(out_dt).reshape(nb, T, H, hd))
            vs_rows.append(proj_s[:, :, 2 * D:3 * D].astype(out_dt).reshape(nb, T, H, hd))
        hp, hp16 = _out_ln(mix_p.reshape(B * S, D), w_out16, layer, hp, ln_g[layer], ln_b[layer], alpha,
                           _row_tile(B * S, OUT_TM))
        hs, hs16 = _out_ln(mix_s.reshape(nb * T, D), w_out16, layer, hs, ln_g[layer], ln_b[layer], alpha,
                           _row_tile(nb * T, OUT_TM))
    return (hp.reshape(B, S, D), hs.reshape(nb, T, D),
            jnp.stack(st_p), jnp.stack(st_s),
            jnp.stack(kp_rows), jnp.stack(vp_rows), jnp.stack(ks_rows), jnp.stack(vs_rows))
```

```python
import functools

import numpy as np
import jax
import jax.numpy as jnp
from jax import lax
from jax.experimental import pallas as pl
from jax.experimental.pallas import tpu as pltpu

F32 = jnp.float32
BF16 = jnp.bfloat16

N_MIXERS = 2
HEADS = 16
CHUNK = 64
ATTN_WINDOW = 8 * CHUNK
MAX_REL = 256
LN_EPS = 1e-5
RMS_EPS = 1e-6
LB_TINY = 1e-30
NEG_BIG = -1e30

LANES = 128
SUBLANES = 8
BF16_ROWS = 16
HGRN_CHUNK = 128
HGRN_HEADS_PROMPT = 16
HGRN_HEADS_SAMPLE = 16
ATTN_TQ = ATTN_WINDOW
ATTN_SUB = 128
ATTN_KW = ATTN_SUB + ATTN_WINDOW
ATTN_HEADS = 8
OUT_TM = 512
OUT_SUB = 128
VMEM_LIMIT = 48 * 1024 * 1024
VMEM_LIMIT_BIG = 58 * 1024 * 1024
LOG2E = 1.4426950408889634

_DIMS_NT = (((1,), (1,)), ((), ()))


def _silu(x):
    h = 0.5 * x
    return h * jnp.tanh(h) + h


def _dot_nt(x, y):
    return lax.dot_general(x, y, _DIMS_NT, preferred_element_type=F32)


def _forget_lower_bound(lb_ref, layer_j):
    lbp = lb_ref[...].astype(F32)
    pe = jnp.exp(lbp - jnp.max(lbp, axis=0, keepdims=True))
    prob = pe / jnp.sum(pe, axis=0, keepdims=True)
    cs = prob[0:1]
    for i in range(1, layer_j + 1):
        cs = cs + prob[i:i + 1]
    return cs - prob[0:1]


def _mm_kernel(x_ref, w_ref, o_ref, w16_ref, *, silu):
    @pl.when(pl.program_id(1) == 0)
    def _():
        w16_ref[...] = w_ref[...].astype(BF16)

    acc = jnp.dot(x_ref[...], w16_ref[...], preferred_element_type=F32)
    o_ref[...] = (_silu(acc) if silu else acc).astype(o_ref.dtype)


def _in_proj(x16, w, layer, col0, n_cols, tm, tn, mode="plain", out_dtype=F32):
    M, K = x16.shape
    assert M % tm == 0 and n_cols % tn == 0 and col0 % tn == 0
    cb0 = col0 // tn
    return pl.pallas_call(
        functools.partial(_mm_kernel, silu=mode == "silu"),
        out_shape=jax.ShapeDtypeStruct((M, n_cols), out_dtype),
        grid=(n_cols // tn, M // tm),
        in_specs=[pl.BlockSpec((tm, K), lambda j, i: (i, 0)),
                  pl.BlockSpec((None, K, tn), lambda j, i: (layer, 0, cb0 + j))],
        out_specs=pl.BlockSpec((tm, tn), lambda j, i: (i, j)),
        scratch_shapes=[pltpu.VMEM((K, tn), BF16)],
        compiler_params=pltpu.CompilerParams(
            dimension_semantics=("arbitrary", "arbitrary"),
            vmem_limit_bytes=VMEM_LIMIT_BIG if tm > 1024 else VMEM_LIMIT),
        name="in_proj_" + mode,
    )(x16, w)


def _layer_norm_rows(y, g, b):
    mu = jnp.mean(y, axis=-1, keepdims=True)
    yc = y - mu
    var = jnp.mean(yc * yc, axis=-1, keepdims=True)
    return yc * lax.rsqrt(var + LN_EPS) * g + b


def _out_ln_kernel(m_ref, w_ref, h_ref, g_ref, b_ref, o_ref, o16_ref, *, alpha, sub):
    for r0 in range(0, m_ref.shape[0], sub):
        rows = slice(r0, r0 + sub)
        y = alpha * h_ref[rows] + jnp.dot(m_ref[rows], w_ref[...], preferred_element_type=F32)
        out = _layer_norm_rows(y, g_ref[...], b_ref[...])
        o_ref[rows] = out
        o16_ref[rows] = out.astype(BF16)


def _out_ln(mix16, w16, layer, h, g, b, alpha, tm):
    M, D = h.shape
    assert M % tm == 0
    row = lambda i: (i, 0)
    const = lambda i: (0, 0)
    return pl.pallas_call(
        functools.partial(_out_ln_kernel, alpha=alpha, sub=min(tm, OUT_SUB)),
        out_shape=(jax.ShapeDtypeStruct((M, D), F32), jax.ShapeDtypeStruct((M, D), BF16)),
        grid=(M // tm,),
        in_specs=[pl.BlockSpec((tm, D), row),
                  pl.BlockSpec((None, D, D), lambda i: (layer, 0, 0), pipeline_mode=pl.Buffered(1)),
                  pl.BlockSpec((tm, D), row), pl.BlockSpec((1, D), const), pl.BlockSpec((1, D), const)],
        out_specs=(pl.BlockSpec((tm, D), row), pl.BlockSpec((tm, D), row)),
        compiler_params=pltpu.CompilerParams(
            dimension_semantics=("arbitrary",), vmem_limit_bytes=VMEM_LIMIT),
        name="out_proj_ln",
    )(mix16, w16, h, g.reshape(1, D), b.reshape(1, D))


def _hgrn_heads(qs, z, v16, gs, lb, ng, st_ref, store_mix, masks, sub, C, heads):
    W = heads * LANES
    head = lambda x, h: x[:, h * LANES:(h + 1) * LANES]
    pair = lambda x, y: [_dot_nt(head(x, h), head(y, h)) for h in range(heads)]
    lb_c = jnp.maximum(lb, LB_TINY)
    e = jnp.exp(-jnp.abs(z))
    pos = z >= 0
    inv = 1.0 / (1.0 + e)
    a = jnp.log(jnp.where(pos, 1.0 + lb_c * e, e + lb_c) * inv)
    kk = (1.0 - lb) * (jnp.where(pos, e, 1.0) * inv)
    qs16 = qs.astype(BF16)
    kk16 = kk.astype(BF16)

    V = C // SUBLANES
    shape3 = (V, SUBLANES, W)
    p3 = a.reshape(shape3)
    qs3 = qs.reshape(shape3)
    kk3 = kk.reshape(shape3)
    parts = [pair(qs16, kk16)]
    for n in (1, 2, 4):
        if n == 1:
            bn = jnp.where((sub & 1) != 0, pltpu.roll(p3, 1, 1), p3)
        elif n == 2:
            bn = jnp.where(sub < 4, jnp.broadcast_to(p3[:, 1:2], shape3), jnp.broadcast_to(p3[:, 5:6], shape3))
        else:
            bn = jnp.broadcast_to(p3[:, 3:4], shape3)
        right = (sub & n) != 0
        ex = jnp.exp(jnp.where(right, p3, bn - p3))
        qn = (qs3 * ex).reshape(C, W).astype(BF16)
        kn = (kk3 * ex).reshape(C, W).astype(BF16)
        parts.append(pair(qn, kn))
        p3 = p3 + jnp.where(right, bn, 0.0)
    p = p3.reshape(C, W)

    n = SUBLANES
    while n < C:
        q_rows, k_rows, p_rows = [], [], []
        for lo in range(0, C, 2 * n):
            mid, hi = lo + n, lo + 2 * n
            bn = p[mid - 1:mid]
            k_left = kk[lo:mid] * jnp.exp(bn - p[lo:mid])
            q_right = qs[mid:hi] * jnp.exp(p[mid:hi])
            if n >= BF16_ROWS:
                q_rows += [qs16[lo:mid], q_right.astype(BF16)]
                k_rows += [k_left.astype(BF16), kk16[mid:hi]]
            else:
                q_rows += [qs[lo:mid], q_right]
                k_rows += [k_left, kk[mid:hi]]
            p_rows += [p[lo:mid], p[mid:hi] + bn]
        parts.append(pair(jnp.concatenate(q_rows, axis=0).astype(BF16),
                          jnp.concatenate(k_rows, axis=0).astype(BF16)))
        p = jnp.concatenate(p_rows, axis=0)
        n *= 2

    b_end = p[C - 1:C, :]
    qe = (qs * jnp.exp(p)).astype(BF16)
    kd = (kk * jnp.exp(b_end - p)).astype(BF16)
    dec = jnp.exp(b_end)
    scale = ng * gs

    for h in range(heads):
        amat = jnp.where(masks[0], parts[0][h], 0.0)
        for k in (1, 2, 3):
            amat = jnp.where(masks[k], parts[k][h], amat)
        n, k = SUBLANES, 4
        while n < C:
            rows = []
            for lo in range(0, C, 2 * n):
                mid, hi = lo + n, lo + 2 * n
                rows.append(amat[lo:mid])
                rows.append(jnp.where(masks[k][mid:hi], parts[k][h][mid:hi], amat[mid:hi]))
            amat = jnp.concatenate(rows, axis=0)
            n, k = 2 * n, k + 1

        st = st_ref[h]
        vh = head(v16, h)
        o = jnp.dot(amat.astype(BF16), vh, preferred_element_type=F32) + _dot_nt(head(qe, h), st.astype(BF16))
        st_ref[h] = st * head(dec, h) + jnp.dot(vh.T, head(kd, h), preferred_element_type=F32)
        ms = jnp.mean(o * o, axis=-1, keepdims=True)
        store_mix(h, o * lax.rsqrt(ms + RMS_EPS) * head(scale, h))


def _hgrn_kernel(*refs, layer_j, chunk, heads, has_s0):
    if has_s0:
        lvl_ref, lb_ref, ng_ref, q_ref, z_ref, v_ref, g_ref, s0_ref, mix_ref, sfin_ref, st_ref = refs
    else:
        lvl_ref, lb_ref, ng_ref, q_ref, z_ref, v_ref, g_ref, mix_ref, sfin_ref, st_ref = refs
    C = chunk
    c = pl.program_id(2)

    @pl.when(c == 0)
    def _():
        for h in range(heads):
            if has_s0:
                st_ref[h] = s0_ref[0, h].T
            else:
                st_ref[h] = jnp.zeros((LANES, LANES), F32)

    ng = ng_ref[...].astype(F32)
    lvl = lvl_ref[...]
    n_levels = C.bit_length()
    masks = [lvl == k for k in range(n_levels)]
    sub = lax.broadcasted_iota(jnp.int32, (1, SUBLANES, heads * LANES), 1)

    lb = _forget_lower_bound(lb_ref, layer_j)

    def store_mix(h, val):
        mix_ref[0, :, h * LANES:(h + 1) * LANES] = val.astype(mix_ref.dtype)

    _hgrn_heads(q_ref[0], z_ref[0], v_ref[0], g_ref[0], lb, ng, st_ref, store_mix, masks, sub, C, heads)

    @pl.when(c == pl.num_programs(2) - 1)
    def _():
        for h in range(heads):
            sfin_ref[0, h] = st_ref[h].T


def _level_ids(C):
    t = np.arange(C)[:, None]
    s = np.arange(C)[None, :]
    x = t ^ s
    lv = np.full((C, C), -1, np.int32)
    lv[x == 0] = 0
    n, k = 1, 1
    while n < C:
        lv[(x >= n) & (x < 2 * n) & (t > s)] = k
        n, k = 2 * n, k + 1
    return jnp.asarray(lv)


def _hgrn_mixer(qs, z, v16, gs, hgrn_lb, norm_g, layer_j, chunk, heads, s0=None):
    nb, T, D = qs.shape
    H = HEADS
    assert D // H == LANES and T % chunk == 0 and H % heads == 0 and chunk % BF16_ROWS == 0
    has_s0 = s0 is not None
    hg = H // heads
    w = heads * LANES
    tok_blk = pl.BlockSpec((1, chunk, w), lambda b, h, c: (b, c, h))
    st_blk = pl.BlockSpec((1, heads, LANES, LANES), lambda b, h, c: (b, h, 0, 0))
    in_specs = [
        pl.BlockSpec((chunk, chunk), lambda b, h, c: (0, 0)),
        pl.BlockSpec((hgrn_lb.shape[0], w), lambda b, h, c: (0, h)),
        pl.BlockSpec((1, w), lambda b, h, c: (0, h)),
        tok_blk, tok_blk, tok_blk, tok_blk,
    ]
    args = [_level_ids(chunk), hgrn_lb, norm_g.reshape(1, D), qs, z, v16, gs]
    if has_s0:
        in_specs.append(st_blk)
        args.append(s0)
    return pl.pallas_call(
        functools.partial(_hgrn_kernel, layer_j=layer_j, chunk=chunk, heads=heads, has_s0=has_s0),
        out_shape=(jax.ShapeDtypeStruct((nb, T, D), BF16),
                   jax.ShapeDtypeStruct((nb, H, LANES, LANES), F32)),
        grid=(nb, hg, T // chunk),
        in_specs=in_specs,
        out_specs=(pl.BlockSpec((1, chunk, w), lambda b, h, c: (b, c, h)), st_blk),
        scratch_shapes=[pltpu.VMEM((heads, LANES, LANES), F32)],
        compiler_params=pltpu.CompilerParams(
            dimension_semantics=("arbitrary", "arbitrary", "arbitrary"), vmem_limit_bytes=VMEM_LIMIT),
        name="hgrn_sample" if has_s0 else "hgrn_prompt",
    )(*args)


def _toeplitz_bias(table, rows, cols):
    H, n_rel = table.shape
    assert n_rel == MAX_REL + CHUNK
    d_hi = ATTN_WINDOW + rows - 1
    d_lo = ATTN_WINDOW - (cols - 1)
    left = d_hi - MAX_REL
    right = -(CHUNK - 1) - d_lo
    assert left >= 0 and right >= 0
    g = jnp.pad(table[:, ::-1], ((0, 0), (left, right)), mode="edge")
    n = rows + cols - 1
    g = jnp.pad(g, ((0, 0), (0, 1)))
    flat = jnp.tile(g, (1, rows))[:, rows - 1: rows - 1 + rows * n]
    return flat.reshape(H, rows, n)[:, :, :cols]


def _softmax2_pv(scores, values):
    m = functools.reduce(jnp.maximum, [jnp.max(s, axis=-1, keepdims=True) for s in scores])
    ps = [jnp.exp2(s - m) for s in scores]
    den = functools.reduce(lambda x, y: x + y, [jnp.sum(p, axis=-1, keepdims=True) for p in ps])
    o = functools.reduce(lambda x, y: x + y,
                         [jnp.dot(p.astype(BF16), v, preferred_element_type=F32) for p, v in zip(ps, values)])
    return o * (1.0 / den)


def _attn_prompt_kernel(bias_ref, q_ref, k_ref, v_ref, g_ref, mix_ref, kprev_ref, vprev_ref, *, qscale, heads):
    i = pl.program_id(2)
    tq = q_ref.shape[1]

    @pl.when(i == 0)
    def _():
        kprev_ref[...] = jnp.zeros_like(kprev_ref)
        vprev_ref[...] = jnp.zeros_like(vprev_ref)

    has_prev = i > 0
    for h in range(heads):
        sl = slice(h * LANES, (h + 1) * LANES)
        bias = bias_ref[h]
        q16 = (q_ref[0, :, sl].astype(F32) * qscale).astype(BF16)
        kcat = jnp.concatenate([kprev_ref[:, sl], k_ref[0, :, sl].astype(BF16)], axis=0)
        vcat_t = jnp.concatenate([vprev_ref[:, sl], v_ref[0, :, sl].astype(BF16)], axis=0).T
        s_all = _dot_nt(kcat, q16)
        p_cols, inv_cols = [], []
        for r0 in range(0, tq, ATTN_SUB):
            n_prev = ATTN_WINDOW - r0
            s = s_all[r0:r0 + ATTN_KW, r0:r0 + ATTN_SUB] + bias
            s = jnp.concatenate([jnp.where(has_prev, s[:n_prev], NEG_BIG), s[n_prev:]], axis=0)
            p = jnp.exp2(s - jnp.max(s, axis=0, keepdims=True))
            inv_cols.append(1.0 / jnp.sum(p, axis=0, keepdims=True))
            pieces = [p.astype(BF16)]
            if r0:
                pieces.insert(0, jnp.zeros((r0, ATTN_SUB), BF16))
            if 2 * tq - ATTN_KW - r0:
                pieces.append(jnp.zeros((2 * tq - ATTN_KW - r0, ATTN_SUB), BF16))
            p_cols.append(jnp.concatenate(pieces, axis=0))
        o_t = jnp.dot(vcat_t, jnp.concatenate(p_cols, axis=1), preferred_element_type=F32)
        o = (o_t * jnp.concatenate(inv_cols, axis=1)).T
        mix_ref[0, :, sl] = (o * _silu(g_ref[0, :, sl].astype(F32))).astype(mix_ref.dtype)

    kprev_ref[...] = k_ref[0].astype(BF16)
    vprev_ref[...] = v_ref[0].astype(BF16)


def _attn_prompt(proj, bias2):
    B, S, D4 = proj.shape
    D = D4 // 4
    H = HEADS
    tq = ATTN_TQ
    heads = ATTN_HEADS
    hg = H // heads
    w = heads * LANES
    assert D // H == LANES and S % tq == 0 and H % heads == 0
    col = lambda off: (lambda b, h, i: (b, i, off * hg + h))
    blk = (1, tq, w)
    return pl.pallas_call(
        functools.partial(_attn_prompt_kernel, qscale=float(LANES) ** -0.5 * LOG2E, heads=heads),
        out_shape=jax.ShapeDtypeStruct((B, S, D), BF16),
        grid=(B, hg, S // tq),
        in_specs=[pl.BlockSpec((heads, ATTN_KW, ATTN_SUB), lambda b, h, i: (h, 0, 0)),
                  pl.BlockSpec(blk, col(0)), pl.BlockSpec(blk, col(1)),
                  pl.BlockSpec(blk, col(2)), pl.BlockSpec(blk, col(3))],
        out_specs=pl.BlockSpec(blk, lambda b, h, i: (b, i, h)),
        scratch_shapes=[pltpu.VMEM((tq, w), BF16), pltpu.VMEM((tq, w), BF16)],
        compiler_params=pltpu.CompilerParams(
            dimension_semantics=("arbitrary", "arbitrary", "arbitrary"), vmem_limit_bytes=VMEM_LIMIT),
        name="attn_prompt",
    )(bias2, proj, proj, proj, proj)


def _attn_sample_kernel(bc_ref, bn_ref, q_ref, kn_ref, vn_ref, g_ref, kc_ref, vc_ref, mix_ref, *, qscale, heads):
    win = kc_ref.shape[1] // heads
    for h in range(heads):
        sl = slice(h * LANES, (h + 1) * LANES)
        kc = kc_ref[0, pl.ds(h, win, stride=heads), :].astype(BF16)
        vc = vc_ref[0, pl.ds(h, win, stride=heads), :].astype(BF16)
        q16 = (q_ref[0, :, sl].astype(F32) * qscale).astype(BF16)
        s_c = _dot_nt(q16, kc) + bc_ref[h]
        s_n = _dot_nt(q16, kn_ref[0, :, sl].astype(BF16)) + bn_ref[h]
        o = _softmax2_pv([s_c, s_n], [vc, vn_ref[0, :, sl].astype(BF16)])
        mix_ref[0, :, sl] = (o * _silu(g_ref[0, :, sl].astype(F32))).astype(mix_ref.dtype)


def _attn_sample(proj, cache_k, cache_v, layer_j, bias_c2, bias_n2):
    nb, T, D4 = proj.shape
    D = D4 // 4
    H = HEADS
    W = cache_k.shape[2] // H
    new = lambda off: (lambda b: (b, 0, off))
    blk = (1, T, D)
    cblk = (None, 1, W * H, LANES)
    return pl.pallas_call(
        functools.partial(_attn_sample_kernel, qscale=float(LANES) ** -0.5 * LOG2E, heads=H),
        out_shape=jax.ShapeDtypeStruct((nb, T, D), BF16),
        grid=(nb,),
        in_specs=[pl.BlockSpec((H, T, W), lambda b: (0, 0, 0)),
                  pl.BlockSpec((H, T, T), lambda b: (0, 0, 0)),
                  pl.BlockSpec(blk, new(0)), pl.BlockSpec(blk, new(1)),
                  pl.BlockSpec(blk, new(2)), pl.BlockSpec(blk, new(3)),
                  pl.BlockSpec(cblk, lambda b: (layer_j, b, 0, 0)),
                  pl.BlockSpec(cblk, lambda b: (layer_j, b, 0, 0))],
        out_specs=pl.BlockSpec(blk, lambda b: (b, 0, 0)),
        compiler_params=pltpu.CompilerParams(
            dimension_semantics=("arbitrary",), vmem_limit_bytes=VMEM_LIMIT),
        name="attn_sample",
    )(bias_c2, bias_n2, proj, proj, proj, proj, cache_k, cache_v)


def _row_tile(m, cap):
    t = cap
    while m % t:
        t //= 2
    return t


def kernel(x_prompt, x_sample, state_hgrn, cache_attn_k, cache_attn_v, w_in, w_out, ln_g, ln_b,
           hgrn_lb, hgrn_norm_g, attn_rel_bias):
    B, S, D = x_prompt.shape
    nb, T, _ = x_sample.shape
    depth = w_in.shape[0]
    H = HEADS
    hd = D // H
    alpha = (2 * depth) ** 0.25
    kv_win = cache_attn_k.shape[2]
    prompt_win = min(ATTN_WINDOW, S)
    assert kv_win == ATTN_WINDOW

    w_out16 = w_out.astype(BF16)
    hp = x_prompt.reshape(B * S, D)
    hs = x_sample.reshape(nb * T, D)
    hp16 = hp.astype(BF16)
    hs16 = hs.astype(BF16)
    tm_p = _row_tile(B * S, 1024)
    tm_s = _row_tile(nb * T, 512)
    tn = 1024
    cache_k = cache_attn_k.reshape(cache_attn_k.shape[0], nb, kv_win * H, hd)
    cache_v = cache_attn_v.reshape(cache_attn_v.shape[0], nb, kv_win * H, hd)

    qc = (np.arange(ATTN_SUB)[:, None] + ATTN_WINDOW) // CHUNK
    kc = np.arange(ATTN_KW)[None, :] // CHUNK
    band_ok = jnp.asarray((kc <= qc) & (kc >= qc - ATTN_WINDOW // CHUNK))

    st_p, st_s, kp_rows, vp_rows, ks_rows, vs_rows = [], [], [], [], [], []
    for layer in range(depth):
        j = layer // N_MIXERS
        if layer % N_MIXERS == 0:
            def sections(x16, nseq, tm):
                sec = lambda y: y.reshape(nseq, -1, D)
                return (sec(_in_proj(x16, w_in, layer, 0, D, tm, tn, "silu")),
                        sec(_in_proj(x16, w_in, layer, D, D, tm, tn, "plain")),
                        sec(_in_proj(x16, w_in, layer, 2 * D, D, tm, tn, "plain", BF16)),
                        sec(_in_proj(x16, w_in, layer, 3 * D, D, tm, tn, "silu")))

            mix_p, s_fin = _hgrn_mixer(*sections(hp16, B, tm_p), hgrn_lb, hgrn_norm_g[j], j, min(HGRN_CHUNK, S),
                                       HGRN_HEADS_PROMPT)
            mix_s, s_new = _hgrn_mixer(*sections(hs16, nb, tm_s), hgrn_lb, hgrn_norm_g[j], j, T,
                                       HGRN_HEADS_SAMPLE, s0=state_hgrn[j])
            st_p.append(s_fin.astype(state_hgrn.dtype))
            st_s.append(s_new.astype(state_hgrn.dtype))
        else:
            proj_p = _in_proj(hp16, w_in, layer, 0, 4 * D, _row_tile(B * S, 2048), tn, "plain",
                              BF16).reshape(B, S, 4 * D)
            proj_s = _in_proj(hs16, w_in, layer, 0, 4 * D, tm_s, tn, "plain", BF16).reshape(nb, T, 4 * D)
            toep2 = _toeplitz_bias(attn_rel_bias[j].astype(F32), ATTN_SUB, ATTN_KW) * LOG2E
            mix_p = _attn_prompt(proj_p, jnp.swapaxes(jnp.where(band_ok, toep2, NEG_BIG), 1, 2))
            mix_s = _attn_sample(proj_s, cache_k, cache_v, j,
                                 toep2[:, :T, :kv_win], toep2[:, :T, kv_win:kv_win + T])
            out_dt = x_prompt.dtype
            kp_rows.append(proj_p[:, S - prompt_win:, D:2 * D].astype(out_dt).reshape(B, prompt_win, H, hd))
            vp_rows.append(proj_p[:, S - prompt_win:, 2 * D:3 * D].astype(out_dt).reshape(B, prompt_win, H, hd))
            ks_rows.append(proj_s[:, :, D:2 * D].astype(out_dt).reshape(nb, T, H, hd))
            vs_rows.append(proj_s[:, :, 2 * D:3 * D].astype(out_dt).reshape(nb, T, H, hd))
        hp, hp16 = _out_ln(mix_p.reshape(B * S, D), w_out16, layer, hp, ln_g[layer], ln_b[layer], alpha,
                           _row_tile(B * S, OUT_TM))
        hs, hs16 = _out_ln(mix_s.reshape(nb * T, D), w_out16, layer, hs, ln_g[layer], ln_b[layer], alpha,
                           _row_tile(nb * T, OUT_TM))
    return (hp.reshape(B, S, D), hs.reshape(nb, T, D),
            jnp.stack(st_p), jnp.stack(st_s),
            jnp.stack(kp_rows), jnp.stack(vp_rows), jnp.stack(ks_rows), jnp.stack(vs_rows))
```

```python
import functools

import numpy as np
import jax
import jax.numpy as jnp
from jax import lax
from jax.experimental import pallas as pl
from jax.experimental.pallas import tpu as pltpu

F32 = jnp.float32
BF16 = jnp.bfloat16

N_MIXERS = 2
HEADS = 16
CHUNK = 64
ATTN_WINDOW = 8 * CHUNK
MAX_REL = 256
LN_EPS = 1e-5
RMS_EPS = 1e-6
LB_TINY = 1e-30
NEG_BIG = -1e30

LANES = 128
SUBLANES = 8
BF16_ROWS = 16
HGRN_CHUNK = 128
HGRN_HEADS_PROMPT = 16
HGRN_HEADS_SAMPLE = 16
HGRN_GROUP = 1
ATTN_TQ = ATTN_WINDOW
ATTN_SUB = 128
ATTN_KW = ATTN_SUB + ATTN_WINDOW
ATTN_HEADS = 8
OUT_TM = 512
OUT_SUB = 128
VMEM_LIMIT = 48 * 1024 * 1024
VMEM_LIMIT_BIG = 58 * 1024 * 1024
LOG2E = 1.4426950408889634

_DIMS_NT = (((1,), (1,)), ((), ()))


def _silu(x):
    h = 0.5 * x
    return h * jnp.tanh(h) + h


def _dot_nt(x, y):
    return lax.dot_general(x, y, _DIMS_NT, preferred_element_type=F32)


def _forget_lower_bound(lb_ref, layer_j):
    lbp = lb_ref[...].astype(F32)
    pe = jnp.exp(lbp - jnp.max(lbp, axis=0, keepdims=True))
    prob = pe / jnp.sum(pe, axis=0, keepdims=True)
    cs = prob[0:1]
    for i in range(1, layer_j + 1):
        cs = cs + prob[i:i + 1]
    return cs - prob[0:1]


def _mm_kernel(x_ref, w_ref, o_ref, w16_ref, *, silu):
    @pl.when(pl.program_id(1) == 0)
    def _():
        w16_ref[...] = w_ref[...].astype(BF16)

    acc = jnp.dot(x_ref[...], w16_ref[...], preferred_element_type=F32)
    o_ref[...] = (_silu(acc) if silu else acc).astype(o_ref.dtype)


def _in_proj(x16, w, layer, col0, n_cols, tm, tn, mode="plain", out_dtype=F32):
    M, K = x16.shape
    assert M % tm == 0 and n_cols % tn == 0 and col0 % tn == 0
    cb0 = col0 // tn
    return pl.pallas_call(
        functools.partial(_mm_kernel, silu=mode == "silu"),
        out_shape=jax.ShapeDtypeStruct((M, n_cols), out_dtype),
        grid=(n_cols // tn, M // tm),
        in_specs=[pl.BlockSpec((tm, K), lambda j, i: (i, 0)),
                  pl.BlockSpec((None, K, tn), lambda j, i: (layer, 0, cb0 + j))],
        out_specs=pl.BlockSpec((tm, tn), lambda j, i: (i, j)),
        scratch_shapes=[pltpu.VMEM((K, tn), BF16)],
        compiler_params=pltpu.CompilerParams(
            dimension_semantics=("arbitrary", "arbitrary"),
            vmem_limit_bytes=VMEM_LIMIT_BIG if tm > 1024 else VMEM_LIMIT),
        name="in_proj_" + mode,
    )(x16, w)


def _layer_norm_rows(y, g, b):
    mu = jnp.mean(y, axis=-1, keepdims=True)
    yc = y - mu
    var = jnp.mean(yc * yc, axis=-1, keepdims=True)
    return yc * lax.rsqrt(var + LN_EPS) * g + b


def _out_ln_kernel(m_ref, w_ref, h_ref, g_ref, b_ref, o_ref, o16_ref, *, alpha, sub):
    for r0 in range(0, m_ref.shape[0], sub):
        rows = slice(r0, r0 + sub)
        y = alpha * h_ref[rows] + jnp.dot(m_ref[rows], w_ref[...], preferred_element_type=F32)
        out = _layer_norm_rows(y, g_ref[...], b_ref[...])
        o_ref[rows] = out
        o16_ref[rows] = out.astype(BF16)


def _out_ln(mix16, w16, layer, h, g, b, alpha, tm):
    M, D = h.shape
    assert M % tm == 0
    row = lambda i: (i, 0)
    const = lambda i: (0, 0)
    return pl.pallas_call(
        functools.partial(_out_ln_kernel, alpha=alpha, sub=min(tm, OUT_SUB)),
        out_shape=(jax.ShapeDtypeStruct((M, D), F32), jax.ShapeDtypeStruct((M, D), BF16)),
        grid=(M // tm,),
        in_specs=[pl.BlockSpec((tm, D), row),
                  pl.BlockSpec((None, D, D), lambda i: (layer, 0, 0), pipeline_mode=pl.Buffered(1)),
                  pl.BlockSpec((tm, D), row), pl.BlockSpec((1, D), const), pl.BlockSpec((1, D), const)],
        out_specs=(pl.BlockSpec((tm, D), row), pl.BlockSpec((tm, D), row)),
        compiler_params=pltpu.CompilerParams(
            dimension_semantics=("arbitrary",), vmem_limit_bytes=VMEM_LIMIT),
        name="out_proj_ln",
    )(mix16, w16, h, g.reshape(1, D), b.reshape(1, D))


def _head(x, h):
    return x[:, h * LANES:(h + 1) * LANES]


def _hgrn_stage1(qs, z, lb, sub, C):
    W = qs.shape[1]
    heads = W // LANES
    pair = lambda x, y: [_dot_nt(_head(x, h), _head(y, h)) for h in range(heads)]
    lb_c = jnp.maximum(lb, LB_TINY)
    e = jnp.exp(-jnp.abs(z))
    pos = z >= 0
    inv = 1.0 / (1.0 + e)
    a = jnp.log2(jnp.where(pos, 1.0 + lb_c * e, e + lb_c) * inv)
    kk = (1.0 - lb) * (jnp.where(pos, e, 1.0) * inv)
    qs16 = qs.astype(BF16)
    kk16 = kk.astype(BF16)

    V = C // SUBLANES
    shape3 = (V, SUBLANES, W)
    p3 = a.reshape(shape3)
    qs3 = qs.reshape(shape3)
    kk3 = kk.reshape(shape3)
    parts = [pair(qs16, kk16)]
    for n in (1, 2, 4):
        if n == 1:
            bn = jnp.where((sub & 1) != 0, pltpu.roll(p3, 1, 1), p3)
        elif n == 2:
            bn = jnp.where(sub < 4, jnp.broadcast_to(p3[:, 1:2], shape3), jnp.broadcast_to(p3[:, 5:6], shape3))
        else:
            bn = jnp.broadcast_to(p3[:, 3:4], shape3)
        right = (sub & n) != 0
        ex = jnp.exp2(jnp.where(right, p3, bn - p3))
        qn = (qs3 * ex).reshape(C, W).astype(BF16)
        kn = (kk3 * ex).reshape(C, W).astype(BF16)
        parts.append(pair(qn, kn))
        p3 = p3 + jnp.where(right, bn, 0.0)
    p = p3.reshape(C, W)

    n = SUBLANES
    while n < C:
        q_rows, k_rows, p_rows = [], [], []
        for lo in range(0, C, 2 * n):
            mid, hi = lo + n, lo + 2 * n
            bn = p[mid - 1:mid]
            k_left = kk[lo:mid] * jnp.exp2(bn - p[lo:mid])
            q_right = qs[mid:hi] * jnp.exp2(p[mid:hi])
            if n >= BF16_ROWS:
                q_rows += [qs16[lo:mid], q_right.astype(BF16)]
                k_rows += [k_left.astype(BF16), kk16[mid:hi]]
            else:
                q_rows += [qs[lo:mid], q_right]
                k_rows += [k_left, kk[mid:hi]]
            p_rows += [p[lo:mid], p[mid:hi] + bn]
        parts.append(pair(jnp.concatenate(q_rows, axis=0).astype(BF16),
                          jnp.concatenate(k_rows, axis=0).astype(BF16)))
        p = jnp.concatenate(p_rows, axis=0)
        n *= 2

    b_end = p[C - 1:C, :]
    qe = (qs * jnp.exp2(p)).astype(BF16)
    kd = (kk * jnp.exp2(b_end - p)).astype(BF16)
    return parts, qe, kd, jnp.exp2(b_end)


def _hgrn_stage2(parts, qe, kd, dec, v16, scale, st_ref, store_mix, masks, C, h0):
    for i in range(qe.shape[1] // LANES):
        h = h0 + i
        amat = jnp.where(masks[0], parts[0][i], 0.0)
        for k in (1, 2, 3):
            amat = jnp.where(masks[k], parts[k][i], amat)
        n, k = SUBLANES, 4
        while n < C:
            rows = []
            for lo in range(0, C, 2 * n):
                mid, hi = lo + n, lo + 2 * n
                rows.append(amat[lo:mid])
                rows.append(jnp.where(masks[k][mid:hi], parts[k][i][mid:hi], amat[mid:hi]))
            amat = jnp.concatenate(rows, axis=0)
            n, k = 2 * n, k + 1

        st = st_ref[h]
        vh = _head(v16, i)
        o = jnp.dot(amat.astype(BF16), vh, preferred_element_type=F32) + _dot_nt(_head(qe, i), st.astype(BF16))
        st_ref[h] = st * _head(dec, i) + jnp.dot(vh.T, _head(kd, i), preferred_element_type=F32)
        ms = jnp.mean(o * o, axis=-1, keepdims=True)
        store_mix(h, o * lax.rsqrt(ms + RMS_EPS) * _head(scale, i))


def _hgrn_kernel(*refs, layer_j, chunk, heads, has_s0):
    if has_s0:
        lvl_ref, lb_ref, ng_ref, q_ref, z_ref, v_ref, g_ref, s0_ref, mix_ref, sfin_ref, st_ref = refs
    else:
        lvl_ref, lb_ref, ng_ref, q_ref, z_ref, v_ref, g_ref, mix_ref, sfin_ref, st_ref = refs
    C = chunk
    c = pl.program_id(2)

    @pl.when(c == 0)
    def _():
        for h in range(heads):
            if has_s0:
                st_ref[h] = s0_ref[0, h].T
            else:
                st_ref[h] = jnp.zeros((LANES, LANES), F32)

    ng = ng_ref[...].astype(F32)
    lvl = lvl_ref[...]
    n_levels = C.bit_length()
    masks = [lvl == k for k in range(n_levels)]
    group = min(HGRN_GROUP, heads)
    gw = group * LANES
    sub = lax.broadcasted_iota(jnp.int32, (1, SUBLANES, gw), 1)
    lb = _forget_lower_bound(lb_ref, layer_j)

    def store_mix(h, val):
        mix_ref[0, :, h * LANES:(h + 1) * LANES] = val.astype(mix_ref.dtype)

    pending = None
    for h0 in range(0, heads, group):
        cols = slice(h0 * LANES, h0 * LANES + gw)
        stage1 = _hgrn_stage1(q_ref[0, :, cols], z_ref[0, :, cols], lb[:, cols], sub, C)
        if pending is not None:
            _hgrn_stage2(*pending)
        pending = (*stage1, v_ref[0, :, cols], ng[:, cols] * g_ref[0, :, cols], st_ref, store_mix, masks, C, h0)
    _hgrn_stage2(*pending)

    @pl.when(c == pl.num_programs(2) - 1)
    def _():
        for h in range(heads):
            sfin_ref[0, h] = st_ref[h].T


def _level_ids(C):
    t = np.arange(C)[:, None]
    s = np.arange(C)[None, :]
    x = t ^ s
    lv = np.full((C, C), -1, np.int32)
    lv[x == 0] = 0
    n, k = 1, 1
    while n < C:
        lv[(x >= n) & (x < 2 * n) & (t > s)] = k
        n, k = 2 * n, k + 1
    return jnp.asarray(lv)


def _hgrn_mixer(qs, z, v16, gs, hgrn_lb, norm_g, layer_j, chunk, heads, s0=None):
    nb, T, D = qs.shape
    H = HEADS
    assert D // H == LANES and T % chunk == 0 and H % heads == 0 and chunk % BF16_ROWS == 0
    has_s0 = s0 is not None
    hg = H // heads
    w = heads * LANES
    tok_blk = pl.BlockSpec((1, chunk, w), lambda b, h, c: (b, c, h))
    st_blk = pl.BlockSpec((1, heads, LANES, LANES), lambda b, h, c: (b, h, 0, 0))
    in_specs = [
        pl.BlockSpec((chunk, chunk), lambda b, h, c: (0, 0)),
        pl.BlockSpec((hgrn_lb.shape[0], w), lambda b, h, c: (0, h)),
        pl.BlockSpec((1, w), lambda b, h, c: (0, h)),
        tok_blk, tok_blk, tok_blk, tok_blk,
    ]
    args = [_level_ids(chunk), hgrn_lb, norm_g.reshape(1, D), qs, z, v16, gs]
    if has_s0:
        in_specs.append(st_blk)
        args.append(s0)
    return pl.pallas_call(
        functools.partial(_hgrn_kernel, layer_j=layer_j, chunk=chunk, heads=heads, has_s0=has_s0),
        out_shape=(jax.ShapeDtypeStruct((nb, T, D), BF16),
                   jax.ShapeDtypeStruct((nb, H, LANES, LANES), F32)),
        grid=(nb, hg, T // chunk),
        in_specs=in_specs,
        out_specs=(pl.BlockSpec((1, chunk, w), lambda b, h, c: (b, c, h)), st_blk),
        scratch_shapes=[pltpu.VMEM((heads, LANES, LANES), F32)],
        compiler_params=pltpu.CompilerParams(
            dimension_semantics=("arbitrary", "arbitrary", "arbitrary"), vmem_limit_bytes=VMEM_LIMIT),
        name="hgrn_sample" if has_s0 else "hgrn_prompt",
    )(*args)


def _toeplitz_bias(table, rows, cols):
    H, n_rel = table.shape
    assert n_rel == MAX_REL + CHUNK
    d_hi = ATTN_WINDOW + rows - 1
    d_lo = ATTN_WINDOW - (cols - 1)
    left = d_hi - MAX_REL
    right = -(CHUNK - 1) - d_lo
    assert left >= 0 and right >= 0
    g = jnp.pad(table[:, ::-1], ((0, 0), (left, right)), mode="edge")
    n = rows + cols - 1
    g = jnp.pad(g, ((0, 0), (0, 1)))
    flat = jnp.tile(g, (1, rows))[:, rows - 1: rows - 1 + rows * n]
    return flat.reshape(H, rows, n)[:, :, :cols]


def _softmax2_pv(scores, values):
    m = functools.reduce(jnp.maximum, [jnp.max(s, axis=-1, keepdims=True) for s in scores])
    ps = [jnp.exp2(s - m) for s in scores]
    den = functools.reduce(lambda x, y: x + y, [jnp.sum(p, axis=-1, keepdims=True) for p in ps])
    o = functools.reduce(lambda x, y: x + y,
                         [jnp.dot(p.astype(BF16), v, preferred_element_type=F32) for p, v in zip(ps, values)])
    return o * (1.0 / den)


def _attn_prompt_kernel(bias_ref, q_ref, k_ref, v_ref, g_ref, mix_ref, kprev_ref, vprev_ref, *, qscale, heads):
    i = pl.program_id(2)
    tq = q_ref.shape[1]

    @pl.when(i == 0)
    def _():
        kprev_ref[...] = jnp.zeros_like(kprev_ref)
        vprev_ref[...] = jnp.zeros_like(vprev_ref)

    has_prev = i > 0
    for h in range(heads):
        sl = slice(h * LANES, (h + 1) * LANES)
        bias = bias_ref[h]
        q16 = (q_ref[0, :, sl].astype(F32) * qscale).astype(BF16)
        kcat = jnp.concatenate([kprev_ref[:, sl], k_ref[0, :, sl].astype(BF16)], axis=0)
        vcat_t = jnp.concatenate([vprev_ref[:, sl], v_ref[0, :, sl].astype(BF16)], axis=0).T
        s_all = _dot_nt(kcat, q16)
        p_cols, inv_cols = [], []
        for r0 in range(0, tq, ATTN_SUB):
            n_prev = ATTN_WINDOW - r0
            s = s_all[r0:r0 + ATTN_KW, r0:r0 + ATTN_SUB] + bias
            s = jnp.concatenate([jnp.where(has_prev, s[:n_prev], NEG_BIG), s[n_prev:]], axis=0)
            p = jnp.exp2(s - jnp.max(s, axis=0, keepdims=True))
            inv_cols.append(1.0 / jnp.sum(p, axis=0, keepdims=True))
            pieces = [p.astype(BF16)]
            if r0:
                pieces.insert(0, jnp.zeros((r0, ATTN_SUB), BF16))
            if 2 * tq - ATTN_KW - r0:
                pieces.append(jnp.zeros((2 * tq - ATTN_KW - r0, ATTN_SUB), BF16))
            p_cols.append(jnp.concatenate(pieces, axis=0))
        o_t = jnp.dot(vcat_t, jnp.concatenate(p_cols, axis=1), preferred_element_type=F32)
        o = (o_t * jnp.concatenate(inv_cols, axis=1)).T
        mix_ref[0, :, sl] = (o * _silu(g_ref[0, :, sl].astype(F32))).astype(mix_ref.dtype)

    kprev_ref[...] = k_ref[0].astype(BF16)
    vprev_ref[...] = v_ref[0].astype(BF16)


def _attn_prompt(proj, bias2):
    B, S, D4 = proj.shape
    D = D4 // 4
    H = HEADS
    tq = ATTN_TQ
    heads = ATTN_HEADS
    hg = H // heads
    w = heads * LANES
    assert D // H == LANES and S % tq == 0 and H % heads == 0
    col = lambda off: (lambda b, h, i: (b, i, off * hg + h))
    blk = (1, tq, w)
    return pl.pallas_call(
        functools.partial(_attn_prompt_kernel, qscale=float(LANES) ** -0.5 * LOG2E, heads=heads),
        out_shape=jax.ShapeDtypeStruct((B, S, D), BF16),
        grid=(B, hg, S // tq),
        in_specs=[pl.BlockSpec((heads, ATTN_KW, ATTN_SUB), lambda b, h, i: (h, 0, 0)),
                  pl.BlockSpec(blk, col(0)), pl.BlockSpec(blk, col(1)),
                  pl.BlockSpec(blk, col(2)), pl.BlockSpec(blk, col(3))],
        out_specs=pl.BlockSpec(blk, lambda b, h, i: (b, i, h)),
        scratch_shapes=[pltpu.VMEM((tq, w), BF16), pltpu.VMEM((tq, w), BF16)],
        compiler_params=pltpu.CompilerParams(
            dimension_semantics=("arbitrary", "arbitrary", "arbitrary"), vmem_limit_bytes=VMEM_LIMIT),
        name="attn_prompt",
    )(bias2, proj, proj, proj, proj)


def _attn_sample_kernel(bc_ref, bn_ref, q_ref, kn_ref, vn_ref, g_ref, kc_ref, vc_ref, mix_ref, *, qscale, heads):
    win = kc_ref.shape[1] // heads
    for h in range(heads):
        sl = slice(h * LANES, (h + 1) * LANES)
        kc = kc_ref[0, pl.ds(h, win, stride=heads), :].astype(BF16)
        vc = vc_ref[0, pl.ds(h, win, stride=heads), :].astype(BF16)
        q16 = (q_ref[0, :, sl].astype(F32) * qscale).astype(BF16)
        s_c = _dot_nt(q16, kc) + bc_ref[h]
        s_n = _dot_nt(q16, kn_ref[0, :, sl].astype(BF16)) + bn_ref[h]
        o = _softmax2_pv([s_c, s_n], [vc, vn_ref[0, :, sl].astype(BF16)])
        mix_ref[0, :, sl] = (o * _silu(g_ref[0, :, sl].astype(F32))).astype(mix_ref.dtype)


def _attn_sample(proj, cache_k, cache_v, layer_j, bias_c2, bias_n2):
    nb, T, D4 = proj.shape
    D = D4 // 4
    H = HEADS
    W = cache_k.shape[2] // H
    new = lambda off: (lambda b: (b, 0, off))
    blk = (1, T, D)
    cblk = (None, 1, W * H, LANES)
    return pl.pallas_call(
        functools.partial(_attn_sample_kernel, qscale=float(LANES) ** -0.5 * LOG2E, heads=H),
        out_shape=jax.ShapeDtypeStruct((nb, T, D), BF16),
        grid=(nb,),
        in_specs=[pl.BlockSpec((H, T, W), lambda b: (0, 0, 0)),
                  pl.BlockSpec((H, T, T), lambda b: (0, 0, 0)),
                  pl.BlockSpec(blk, new(0)), pl.BlockSpec(blk, new(1)),
                  pl.BlockSpec(blk, new(2)), pl.BlockSpec(blk, new(3)),
                  pl.BlockSpec(cblk, lambda b: (layer_j, b, 0, 0)),
                  pl.BlockSpec(cblk, lambda b: (layer_j, b, 0, 0))],
        out_specs=pl.BlockSpec(blk, lambda b: (b, 0, 0)),
        compiler_params=pltpu.CompilerParams(
            dimension_semantics=("arbitrary",), vmem_limit_bytes=VMEM_LIMIT),
        name="attn_sample",
    )(bias_c2, bias_n2, proj, proj, proj, proj, cache_k, cache_v)


def _row_tile(m, cap):
    t = cap
    while m % t:
        t //= 2
    return t


def kernel(x_prompt, x_sample, state_hgrn, cache_attn_k, cache_attn_v, w_in, w_out, ln_g, ln_b,
           hgrn_lb, hgrn_norm_g, attn_rel_bias):
    B, S, D = x_prompt.shape
    nb, T, _ = x_sample.shape
    depth = w_in.shape[0]
    H = HEADS
    hd = D // H
    alpha = (2 * depth) ** 0.25
    kv_win = cache_attn_k.shape[2]
    prompt_win = min(ATTN_WINDOW, S)
    assert kv_win == ATTN_WINDOW

    w_out16 = w_out.astype(BF16)
    hp = x_prompt.reshape(B * S, D)
    hs = x_sample.reshape(nb * T, D)
    hp16 = hp.astype(BF16)
    hs16 = hs.astype(BF16)
    tm_p = _row_tile(B * S, 1024)
    tm_s = _row_tile(nb * T, 512)
    tn = 1024
    cache_k = cache_attn_k.reshape(cache_attn_k.shape[0], nb, kv_win * H, hd)
    cache_v = cache_attn_v.reshape(cache_attn_v.shape[0], nb, kv_win * H, hd)

    qc = (np.arange(ATTN_SUB)[:, None] + ATTN_WINDOW) // CHUNK
    kc = np.arange(ATTN_KW)[None, :] // CHUNK
    band_ok = jnp.asarray((kc <= qc) & (kc >= qc - ATTN_WINDOW // CHUNK))

    st_p, st_s, kp_rows, vp_rows, ks_rows, vs_rows = [], [], [], [], [], []
    for layer in range(depth):
        j = layer // N_MIXERS
        if layer % N_MIXERS == 0:
            def sections(x16, nseq, tm):
                sec = lambda y: y.reshape(nseq, -1, D)
                return (sec(_in_proj(x16, w_in, layer, 0, D, tm, tn, "silu")),
                        sec(_in_proj(x16, w_in, layer, D, D, tm, tn, "plain")),
                        sec(_in_proj(x16, w_in, layer, 2 * D, D, tm, tn, "plain", BF16)),
                        sec(_in_proj(x16, w_in, layer, 3 * D, D, tm, tn, "silu")))

            mix_p, s_fin = _hgrn_mixer(*sections(hp16, B, tm_p), hgrn_lb, hgrn_norm_g[j], j, min(HGRN_CHUNK, S),
                                       HGRN_HEADS_PROMPT)
            mix_s, s_new = _hgrn_mixer(*sections(hs16, nb, tm_s), hgrn_lb, hgrn_norm_g[j], j, T,
                                       HGRN_HEADS_SAMPLE, s0=state_hgrn[j])
            st_p.append(s_fin.astype(state_hgrn.dtype))
            st_s.append(s_new.astype(state_hgrn.dtype))
        else:
            proj_p = _in_proj(hp16, w_in, layer, 0, 4 * D, _row_tile(B * S, 2048), tn, "plain",
                              BF16).reshape(B, S, 4 * D)
            proj_s = _in_proj(hs16, w_in, layer, 0, 4 * D, tm_s, tn, "plain", BF16).reshape(nb, T, 4 * D)
            toep2 = _toeplitz_bias(attn_rel_bias[j].astype(F32), ATTN_SUB, ATTN_KW) * LOG2E
            mix_p = _attn_prompt(proj_p, jnp.swapaxes(jnp.where(band_ok, toep2, NEG_BIG), 1, 2))
            mix_s = _attn_sample(proj_s, cache_k, cache_v, j,
                                 toep2[:, :T, :kv_win], toep2[:, :T, kv_win:kv_win + T])
            out_dt = x_prompt.dtype
            kp_rows.append(proj_p[:, S - prompt_win:, D:2 * D].astype(out_dt).reshape(B, prompt_win, H, hd))
            vp_rows.append(proj_p[:, S - prompt_win:, 2 * D:3 * D].astype(out_dt).reshape(B, prompt_win, H, hd))
            ks_rows.append(proj_s[:, :, D:2 * D].astype(out_dt).reshape(nb, T, H, hd))
            vs_rows.append(proj_s[:, :, 2 * D:3 * D].astype(out_dt).reshape(nb, T, H, hd))
        hp, hp16 = _out_ln(mix_p.reshape(B * S, D), w_out16, layer, hp, ln_g[layer], ln_b[layer], alpha,
                           _row_tile(B * S, OUT_TM))
        hs, hs16 = _out_ln(mix_s.reshape(nb * T, D), w_out16, layer, hs, ln_g[layer], ln_b[layer], alpha,
                           _row_tile(nb * T, OUT_TM))
    return (hp.reshape(B, S, D), hs.reshape(nb, T, D),
            jnp.stack(st_p), jnp.stack(st_s),
            jnp.stack(kp_rows), jnp.stack(vp_rows), jnp.stack(ks_rows), jnp.stack(vs_rows))
```

```python
import functools

import numpy as np
import jax
import jax.numpy as jnp
from jax import lax
from jax.experimental import pallas as pl
from jax.experimental.pallas import tpu as pltpu

F32 = jnp.float32
BF16 = jnp.bfloat16

N_MIXERS = 2
HEADS = 16
CHUNK = 64
ATTN_WINDOW = 8 * CHUNK
MAX_REL = 256
LN_EPS = 1e-5
RMS_EPS = 1e-6
LB_TINY = 1e-30
NEG_BIG = -1e30

LANES = 128
SUBLANES = 8
BF16_ROWS = 16
HGRN_CHUNK = 128
HGRN_HEADS_PROMPT = 16
HGRN_HEADS_SAMPLE = 16
HGRN_GROUP = 1
ATTN_TQ = ATTN_WINDOW
ATTN_SUB = 128
ATTN_KW = ATTN_SUB + ATTN_WINDOW
ATTN_HEADS = 16
OUT_TM = 512
OUT_SUB = 128
VMEM_LIMIT = 48 * 1024 * 1024
VMEM_LIMIT_BIG = 58 * 1024 * 1024
LOG2E = 1.4426950408889634

_DIMS_NT = (((1,), (1,)), ((), ()))


def _silu(x):
    h = 0.5 * x
    return h * jnp.tanh(h) + h


def _dot_nt(x, y):
    return lax.dot_general(x, y, _DIMS_NT, preferred_element_type=F32)


def _forget_lower_bound(lb_ref, layer_j):
    lbp = lb_ref[...].astype(F32)
    pe = jnp.exp(lbp - jnp.max(lbp, axis=0, keepdims=True))
    prob = pe / jnp.sum(pe, axis=0, keepdims=True)
    cs = prob[0:1]
    for i in range(1, layer_j + 1):
        cs = cs + prob[i:i + 1]
    return cs - prob[0:1]


def _mm_kernel(x_ref, w_ref, o_ref, w16_ref, *, silu):
    @pl.when(pl.program_id(1) == 0)
    def _():
        w16_ref[...] = w_ref[...].astype(BF16)

    acc = jnp.dot(x_ref[...], w16_ref[...], preferred_element_type=F32)
    o_ref[...] = (_silu(acc) if silu else acc).astype(o_ref.dtype)


def _in_proj(x16, w, layer, col0, n_cols, tm, tn, mode="plain", out_dtype=F32):
    M, K = x16.shape
    assert M % tm == 0 and n_cols % tn == 0 and col0 % tn == 0
    cb0 = col0 // tn
    return pl.pallas_call(
        functools.partial(_mm_kernel, silu=mode == "silu"),
        out_shape=jax.ShapeDtypeStruct((M, n_cols), out_dtype),
        grid=(n_cols // tn, M // tm),
        in_specs=[pl.BlockSpec((tm, K), lambda j, i: (i, 0)),
                  pl.BlockSpec((None, K, tn), lambda j, i: (layer, 0, cb0 + j))],
        out_specs=pl.BlockSpec((tm, tn), lambda j, i: (i, j)),
        scratch_shapes=[pltpu.VMEM((K, tn), BF16)],
        compiler_params=pltpu.CompilerParams(
            dimension_semantics=("arbitrary", "arbitrary"),
            vmem_limit_bytes=VMEM_LIMIT_BIG if tm > 1024 else VMEM_LIMIT),
        name="in_proj_" + mode,
    )(x16, w)


def _layer_norm_rows(y, g, b):
    mu = jnp.mean(y, axis=-1, keepdims=True)
    yc = y - mu
    var = jnp.mean(yc * yc, axis=-1, keepdims=True)
    return yc * lax.rsqrt(var + LN_EPS) * g + b


def _out_ln_kernel(m_ref, w_ref, h_ref, g_ref, b_ref, o_ref, o16_ref, *, alpha, sub):
    for r0 in range(0, m_ref.shape[0], sub):
        rows = slice(r0, r0 + sub)
        y = alpha * h_ref[rows] + jnp.dot(m_ref[rows], w_ref[...], preferred_element_type=F32)
        out = _layer_norm_rows(y, g_ref[...], b_ref[...])
        o_ref[rows] = out
        o16_ref[rows] = out.astype(BF16)


def _out_ln(mix16, w16, layer, h, g, b, alpha, tm):
    M, D = h.shape
    assert M % tm == 0
    row = lambda i: (i, 0)
    const = lambda i: (0, 0)
    return pl.pallas_call(
        functools.partial(_out_ln_kernel, alpha=alpha, sub=min(tm, OUT_SUB)),
        out_shape=(jax.ShapeDtypeStruct((M, D), F32), jax.ShapeDtypeStruct((M, D), BF16)),
        grid=(M // tm,),
        in_specs=[pl.BlockSpec((tm, D), row),
                  pl.BlockSpec((None, D, D), lambda i: (layer, 0, 0), pipeline_mode=pl.Buffered(1)),
                  pl.BlockSpec((tm, D), row), pl.BlockSpec((1, D), const), pl.BlockSpec((1, D), const)],
        out_specs=(pl.BlockSpec((tm, D), row), pl.BlockSpec((tm, D), row)),
        compiler_params=pltpu.CompilerParams(
            dimension_semantics=("arbitrary",), vmem_limit_bytes=VMEM_LIMIT),
        name="out_proj_ln",
    )(mix16, w16, h, g.reshape(1, D), b.reshape(1, D))


def _head(x, h):
    return x[:, h * LANES:(h + 1) * LANES]


def _hgrn_stage1(qs, z, lb, sub, C):
    W = qs.shape[1]
    heads = W // LANES
    pair = lambda x, y: [_dot_nt(_head(x, h), _head(y, h)) for h in range(heads)]
    lb_c = jnp.maximum(lb, LB_TINY)
    e = jnp.exp(-jnp.abs(z))
    pos = z >= 0
    inv = 1.0 / (1.0 + e)
    a = jnp.log2(jnp.where(pos, 1.0 + lb_c * e, e + lb_c) * inv)
    kk = (1.0 - lb) * (jnp.where(pos, e, 1.0) * inv)
    qs16 = qs.astype(BF16)
    kk16 = kk.astype(BF16)

    V = C // SUBLANES
    shape3 = (V, SUBLANES, W)
    p3 = a.reshape(shape3)
    qs3 = qs.reshape(shape3)
    kk3 = kk.reshape(shape3)
    parts = [pair(qs16, kk16)]
    for n in (1, 2, 4):
        if n == 1:
            bn = jnp.where((sub & 1) != 0, pltpu.roll(p3, 1, 1), p3)
        elif n == 2:
            bn = jnp.where(sub < 4, jnp.broadcast_to(p3[:, 1:2], shape3), jnp.broadcast_to(p3[:, 5:6], shape3))
        else:
            bn = jnp.broadcast_to(p3[:, 3:4], shape3)
        right = (sub & n) != 0
        ex = jnp.exp2(jnp.where(right, p3, bn - p3))
        qn = (qs3 * ex).reshape(C, W).astype(BF16)
        kn = (kk3 * ex).reshape(C, W).astype(BF16)
        parts.append(pair(qn, kn))
        p3 = p3 + jnp.where(right, bn, 0.0)
    p = p3.reshape(C, W)

    n = SUBLANES
    while n < C:
        q_rows, k_rows, p_rows = [], [], []
        for lo in range(0, C, 2 * n):
            mid, hi = lo + n, lo + 2 * n
            bn = p[mid - 1:mid]
            k_left = kk[lo:mid] * jnp.exp2(bn - p[lo:mid])
            q_right = qs[mid:hi] * jnp.exp2(p[mid:hi])
            if n >= BF16_ROWS:
                q_rows += [qs16[lo:mid], q_right.astype(BF16)]
                k_rows += [k_left.astype(BF16), kk16[mid:hi]]
            else:
                q_rows += [qs[lo:mid], q_right]
                k_rows += [k_left, kk[mid:hi]]
            p_rows += [p[lo:mid], p[mid:hi] + bn]
        parts.append(pair(jnp.concatenate(q_rows, axis=0).astype(BF16),
                          jnp.concatenate(k_rows, axis=0).astype(BF16)))
        p = jnp.concatenate(p_rows, axis=0)
        n *= 2

    b_end = p[C - 1:C, :]
    qe = (qs * jnp.exp2(p)).astype(BF16)
    kd = (kk * jnp.exp2(b_end - p)).astype(BF16)
    return parts, qe, kd, jnp.exp2(b_end)


def _hgrn_stage2(parts, qe, kd, dec, v16, scale, st_ref, store_mix, masks, C, h0):
    for i in range(qe.shape[1] // LANES):
        h = h0 + i
        amat = jnp.where(masks[0], parts[0][i], 0.0)
        for k in (1, 2, 3):
            amat = jnp.where(masks[k], parts[k][i], amat)
        n, k = SUBLANES, 4
        while n < C:
            rows = []
            for lo in range(0, C, 2 * n):
                mid, hi = lo + n, lo + 2 * n
                rows.append(amat[lo:mid])
                rows.append(jnp.where(masks[k][mid:hi], parts[k][i][mid:hi], amat[mid:hi]))
            amat = jnp.concatenate(rows, axis=0)
            n, k = 2 * n, k + 1

        st = st_ref[h]
        vh = _head(v16, i)
        o = jnp.dot(amat.astype(BF16), vh, preferred_element_type=F32) + _dot_nt(_head(qe, i), st.astype(BF16))
        st_ref[h] = st * _head(dec, i) + jnp.dot(vh.T, _head(kd, i), preferred_element_type=F32)
        ms = jnp.mean(o * o, axis=-1, keepdims=True)
        store_mix(h, o * lax.rsqrt(ms + RMS_EPS) * _head(scale, i))


def _hgrn_kernel(*refs, layer_j, chunk, heads, has_s0):
    if has_s0:
        lvl_ref, lb_ref, ng_ref, q_ref, z_ref, v_ref, g_ref, s0_ref, mix_ref, sfin_ref, st_ref = refs
    else:
        lvl_ref, lb_ref, ng_ref, q_ref, z_ref, v_ref, g_ref, mix_ref, sfin_ref, st_ref = refs
    C = chunk
    c = pl.program_id(2)

    @pl.when(c == 0)
    def _():
        for h in range(heads):
            if has_s0:
                st_ref[h] = s0_ref[0, h].T
            else:
                st_ref[h] = jnp.zeros((LANES, LANES), F32)

    ng = ng_ref[...].astype(F32)
    lvl = lvl_ref[...]
    n_levels = C.bit_length()
    masks = [lvl == k for k in range(n_levels)]
    group = min(HGRN_GROUP, heads)
    gw = group * LANES
    sub = lax.broadcasted_iota(jnp.int32, (1, SUBLANES, gw), 1)
    lb = _forget_lower_bound(lb_ref, layer_j)

    def store_mix(h, val):
        mix_ref[0, :, h * LANES:(h + 1) * LANES] = val.astype(mix_ref.dtype)

    pending = None
    for h0 in range(0, heads, group):
        cols = slice(h0 * LANES, h0 * LANES + gw)
        stage1 = _hgrn_stage1(q_ref[0, :, cols], z_ref[0, :, cols], lb[:, cols], sub, C)
        if pending is not None:
            _hgrn_stage2(*pending)
        pending = (*stage1, v_ref[0, :, cols], ng[:, cols] * g_ref[0, :, cols], st_ref, store_mix, masks, C, h0)
    _hgrn_stage2(*pending)

    @pl.when(c == pl.num_programs(2) - 1)
    def _():
        for h in range(heads):
            sfin_ref[0, h] = st_ref[h].T


def _level_ids(C):
    t = np.arange(C)[:, None]
    s = np.arange(C)[None, :]
    x = t ^ s
    lv = np.full((C, C), -1, np.int32)
    lv[x == 0] = 0
    n, k = 1, 1
    while n < C:
        lv[(x >= n) & (x < 2 * n) & (t > s)] = k
        n, k = 2 * n, k + 1
    return jnp.asarray(lv)


def _hgrn_mixer(qs, z, v16, gs, hgrn_lb, norm_g, layer_j, chunk, heads, s0=None):
    nb, T, D = qs.shape
    H = HEADS
    assert D // H == LANES and T % chunk == 0 and H % heads == 0 and chunk % BF16_ROWS == 0
    has_s0 = s0 is not None
    hg = H // heads
    w = heads * LANES
    tok_blk = pl.BlockSpec((1, chunk, w), lambda b, h, c: (b, c, h))
    st_blk = pl.BlockSpec((1, heads, LANES, LANES), lambda b, h, c: (b, h, 0, 0))
    in_specs = [
        pl.BlockSpec((chunk, chunk), lambda b, h, c: (0, 0)),
        pl.BlockSpec((hgrn_lb.shape[0], w), lambda b, h, c: (0, h)),
        pl.BlockSpec((1, w), lambda b, h, c: (0, h)),
        tok_blk, tok_blk, tok_blk, tok_blk,
    ]
    args = [_level_ids(chunk), hgrn_lb, norm_g.reshape(1, D), qs, z, v16, gs]
    if has_s0:
        in_specs.append(st_blk)
        args.append(s0)
    return pl.pallas_call(
        functools.partial(_hgrn_kernel, layer_j=layer_j, chunk=chunk, heads=heads, has_s0=has_s0),
        out_shape=(jax.ShapeDtypeStruct((nb, T, D), BF16),
                   jax.ShapeDtypeStruct((nb, H, LANES, LANES), F32)),
        grid=(nb, hg, T // chunk),
        in_specs=in_specs,
        out_specs=(pl.BlockSpec((1, chunk, w), lambda b, h, c: (b, c, h)), st_blk),
        scratch_shapes=[pltpu.VMEM((heads, LANES, LANES), F32)],
        compiler_params=pltpu.CompilerParams(
            dimension_semantics=("arbitrary", "arbitrary", "arbitrary"), vmem_limit_bytes=VMEM_LIMIT),
        name="hgrn_sample" if has_s0 else "hgrn_prompt",
    )(*args)


def _toeplitz_bias(table, rows, cols):
    H, n_rel = table.shape
    assert n_rel == MAX_REL + CHUNK
    d_hi = ATTN_WINDOW + rows - 1
    d_lo = ATTN_WINDOW - (cols - 1)
    left = d_hi - MAX_REL
    right = -(CHUNK - 1) - d_lo
    assert left >= 0 and right >= 0
    g = jnp.pad(table[:, ::-1], ((0, 0), (left, right)), mode="edge")
    n = rows + cols - 1
    g = jnp.pad(g, ((0, 0), (0, 1)))
    flat = jnp.tile(g, (1, rows))[:, rows - 1: rows - 1 + rows * n]
    return flat.reshape(H, rows, n)[:, :, :cols]


def _softmax2_pv(scores, values):
    m = functools.reduce(jnp.maximum, [jnp.max(s, axis=-1, keepdims=True) for s in scores])
    ps = [jnp.exp2(s - m) for s in scores]
    den = functools.reduce(lambda x, y: x + y, [jnp.sum(p, axis=-1, keepdims=True) for p in ps])
    o = functools.reduce(lambda x, y: x + y,
                         [jnp.dot(p.astype(BF16), v, preferred_element_type=F32) for p, v in zip(ps, values)])
    return o * (1.0 / den)


def _attn_prompt_kernel(bias_ref, q_ref, k_ref, v_ref, g_ref, mix_ref, kprev_ref, vprev_ref, *, qscale, heads):
    i = pl.program_id(2)
    tq = q_ref.shape[1]

    @pl.when(i == 0)
    def _():
        kprev_ref[...] = jnp.zeros_like(kprev_ref)
        vprev_ref[...] = jnp.zeros_like(vprev_ref)

    has_prev = i > 0

    def scores(h):
        sl = slice(h * LANES, (h + 1) * LANES)
        q16 = (q_ref[0, :, sl].astype(F32) * qscale).astype(BF16)
        kcat = jnp.concatenate([kprev_ref[:, sl], k_ref[0, :, sl].astype(BF16)], axis=0)
        return _dot_nt(kcat, q16)

    s_next = scores(0)
    for h in range(heads):
        sl = slice(h * LANES, (h + 1) * LANES)
        s_all = s_next
        if h + 1 < heads:
            s_next = scores(h + 1)
        bias = bias_ref[h]
        vcat_t = jnp.concatenate([vprev_ref[:, sl], v_ref[0, :, sl].astype(BF16)], axis=0).T
        p_cols, inv_cols = [], []
        for r0 in range(0, tq, ATTN_SUB):
            n_prev = ATTN_WINDOW - r0
            s = s_all[r0:r0 + ATTN_KW, r0:r0 + ATTN_SUB] + bias
            s = jnp.concatenate([jnp.where(has_prev, s[:n_prev], NEG_BIG), s[n_prev:]], axis=0)
            p = jnp.exp2(s - jnp.max(s, axis=0, keepdims=True))
            inv_cols.append(1.0 / jnp.sum(p, axis=0, keepdims=True))
            pieces = [p.astype(BF16)]
            if r0:
                pieces.insert(0, jnp.zeros((r0, ATTN_SUB), BF16))
            if 2 * tq - ATTN_KW - r0:
                pieces.append(jnp.zeros((2 * tq - ATTN_KW - r0, ATTN_SUB), BF16))
            p_cols.append(jnp.concatenate(pieces, axis=0))
        o_t = jnp.dot(vcat_t, jnp.concatenate(p_cols, axis=1), preferred_element_type=F32)
        o = (o_t * jnp.concatenate(inv_cols, axis=1)).T
        mix_ref[0, :, sl] = (o * _silu(g_ref[0, :, sl].astype(F32))).astype(mix_ref.dtype)

    kprev_ref[...] = k_ref[0].astype(BF16)
    vprev_ref[...] = v_ref[0].astype(BF16)


def _attn_prompt(proj, bias2):
    B, S, D4 = proj.shape
    D = D4 // 4
    H = HEADS
    tq = ATTN_TQ
    heads = ATTN_HEADS
    hg = H // heads
    w = heads * LANES
    assert D // H == LANES and S % tq == 0 and H % heads == 0
    col = lambda off: (lambda b, h, i: (b, i, off * hg + h))
    blk = (1, tq, w)
    return pl.pallas_call(
        functools.partial(_attn_prompt_kernel, qscale=float(LANES) ** -0.5 * LOG2E, heads=heads),
        out_shape=jax.ShapeDtypeStruct((B, S, D), BF16),
        grid=(B, hg, S // tq),
        in_specs=[pl.BlockSpec((heads, ATTN_KW, ATTN_SUB), lambda b, h, i: (h, 0, 0)),
                  pl.BlockSpec(blk, col(0)), pl.BlockSpec(blk, col(1)),
                  pl.BlockSpec(blk, col(2)), pl.BlockSpec(blk, col(3))],
        out_specs=pl.BlockSpec(blk, lambda b, h, i: (b, i, h)),
        scratch_shapes=[pltpu.VMEM((tq, w), BF16), pltpu.VMEM((tq, w), BF16)],
        compiler_params=pltpu.CompilerParams(
            dimension_semantics=("arbitrary", "arbitrary", "arbitrary"), vmem_limit_bytes=VMEM_LIMIT),
        name="attn_prompt",
    )(bias2, proj, proj, proj, proj)


def _attn_sample_kernel(bc_ref, bn_ref, q_ref, kn_ref, vn_ref, g_ref, kc_ref, vc_ref, mix_ref, *, qscale, heads):
    win = kc_ref.shape[1] // heads
    for h in range(heads):
        sl = slice(h * LANES, (h + 1) * LANES)
        kc = kc_ref[0, pl.ds(h, win, stride=heads), :].astype(BF16)
        vc = vc_ref[0, pl.ds(h, win, stride=heads), :].astype(BF16)
        q16 = (q_ref[0, :, sl].astype(F32) * qscale).astype(BF16)
        s_c = _dot_nt(q16, kc) + bc_ref[h]
        s_n = _dot_nt(q16, kn_ref[0, :, sl].astype(BF16)) + bn_ref[h]
        o = _softmax2_pv([s_c, s_n], [vc, vn_ref[0, :, sl].astype(BF16)])
        mix_ref[0, :, sl] = (o * _silu(g_ref[0, :, sl].astype(F32))).astype(mix_ref.dtype)


def _attn_sample(proj, cache_k, cache_v, layer_j, bias_c2, bias_n2):
    nb, T, D4 = proj.shape
    D = D4 // 4
    H = HEADS
    W = cache_k.shape[2] // H
    new = lambda off: (lambda b: (b, 0, off))
    blk = (1, T, D)
    cblk = (None, 1, W * H, LANES)
    return pl.pallas_call(
        functools.partial(_attn_sample_kernel, qscale=float(LANES) ** -0.5 * LOG2E, heads=H),
        out_shape=jax.ShapeDtypeStruct((nb, T, D), BF16),
        grid=(nb,),
        in_specs=[pl.BlockSpec((H, T, W), lambda b: (0, 0, 0)),
                  pl.BlockSpec((H, T, T), lambda b: (0, 0, 0)),
                  pl.BlockSpec(blk, new(0)), pl.BlockSpec(blk, new(1)),
                  pl.BlockSpec(blk, new(2)), pl.BlockSpec(blk, new(3)),
                  pl.BlockSpec(cblk, lambda b: (layer_j, b, 0, 0)),
                  pl.BlockSpec(cblk, lambda b: (layer_j, b, 0, 0))],
        out_specs=pl.BlockSpec(blk, lambda b: (b, 0, 0)),
        compiler_params=pltpu.CompilerParams(
            dimension_semantics=("arbitrary",), vmem_limit_bytes=VMEM_LIMIT),
        name="attn_sample",
    )(bias_c2, bias_n2, proj, proj, proj, proj, cache_k, cache_v)


def _row_tile(m, cap):
    t = cap
    while m % t:
        t //= 2
    return t


def kernel(x_prompt, x_sample, state_hgrn, cache_attn_k, cache_attn_v, w_in, w_out, ln_g, ln_b,
           hgrn_lb, hgrn_norm_g, attn_rel_bias):
    B, S, D = x_prompt.shape
    nb, T, _ = x_sample.shape
    depth = w_in.shape[0]
    H = HEADS
    hd = D // H
    alpha = (2 * depth) ** 0.25
    kv_win = cache_attn_k.shape[2]
    prompt_win = min(ATTN_WINDOW, S)
    assert kv_win == ATTN_WINDOW

    w_out16 = w_out.astype(BF16)
    hp = x_prompt.reshape(B * S, D)
    hs = x_sample.reshape(nb * T, D)
    hp16 = hp.astype(BF16)
    hs16 = hs.astype(BF16)
    tm_p = _row_tile(B * S, 1024)
    tm_s = _row_tile(nb * T, 512)
    tn = 1024
    cache_k = cache_attn_k.reshape(cache_attn_k.shape[0], nb, kv_win * H, hd)
    cache_v = cache_attn_v.reshape(cache_attn_v.shape[0], nb, kv_win * H, hd)

    qc = (np.arange(ATTN_SUB)[:, None] + ATTN_WINDOW) // CHUNK
    kc = np.arange(ATTN_KW)[None, :] // CHUNK
    band_ok = jnp.asarray((kc <= qc) & (kc >= qc - ATTN_WINDOW // CHUNK))

    st_p, st_s, kp_rows, vp_rows, ks_rows, vs_rows = [], [], [], [], [], []
    for layer in range(depth):
        j = layer // N_MIXERS
        if layer % N_MIXERS == 0:
            def sections(x16, nseq, tm):
                sec = lambda y: y.reshape(nseq, -1, D)
                return (sec(_in_proj(x16, w_in, layer, 0, D, tm, tn, "silu")),
                        sec(_in_proj(x16, w_in, layer, D, D, tm, tn, "plain")),
                        sec(_in_proj(x16, w_in, layer, 2 * D, D, tm, tn, "plain", BF16)),
                        sec(_in_proj(x16, w_in, layer, 3 * D, D, tm, tn, "silu")))

            mix_p, s_fin = _hgrn_mixer(*sections(hp16, B, tm_p), hgrn_lb, hgrn_norm_g[j], j, min(HGRN_CHUNK, S),
                                       HGRN_HEADS_PROMPT)
            mix_s, s_new = _hgrn_mixer(*sections(hs16, nb, tm_s), hgrn_lb, hgrn_norm_g[j], j, T,
                                       HGRN_HEADS_SAMPLE, s0=state_hgrn[j])
            st_p.append(s_fin.astype(state_hgrn.dtype))
            st_s.append(s_new.astype(state_hgrn.dtype))
        else:
            proj_p = _in_proj(hp16, w_in, layer, 0, 4 * D, _row_tile(B * S, 2048), tn, "plain",
                              BF16).reshape(B, S, 4 * D)
            proj_s = _in_proj(hs16, w_in, layer, 0, 4 * D, tm_s, tn, "plain", BF16).reshape(nb, T, 4 * D)
            toep2 = _toeplitz_bias(attn_rel_bias[j].astype(F32), ATTN_SUB, ATTN_KW) * LOG2E
            mix_p = _attn_prompt(proj_p, jnp.swapaxes(jnp.where(band_ok, toep2, NEG_BIG), 1, 2))
            mix_s = _attn_sample(proj_s, cache_k, cache_v, j,
                                 toep2[:, :T, :kv_win], toep2[:, :T, kv_win:kv_win + T])
            out_dt = x_prompt.dtype
            kp_rows.append(proj_p[:, S - prompt_win:, D:2 * D].astype(out_dt).reshape(B, prompt_win, H, hd))
            vp_rows.append(proj_p[:, S - prompt_win:, 2 * D:3 * D].astype(out_dt).reshape(B, prompt_win, H, hd))
            ks_rows.append(proj_s[:, :, D:2 * D].astype(out_dt).reshape(nb, T, H, hd))
            vs_rows.append(proj_s[:, :, 2 * D:3 * D].astype(out_dt).reshape(nb, T, H, hd))
        hp, hp16 = _out_ln(mix_p.reshape(B * S, D), w_out16, layer, hp, ln_g[layer], ln_b[layer], alpha,
                           _row_tile(B * S, OUT_TM))
        hs, hs16 = _out_ln(mix_s.reshape(nb * T, D), w_out16, layer, hs, ln_g[layer], ln_b[layer], alpha,
                           _row_tile(nb * T, OUT_TM))
    return (hp.reshape(B, S, D), hs.reshape(nb, T, D),
            jnp.stack(st_p), jnp.stack(st_s),
            jnp.stack(kp_rows), jnp.stack(vp_rows), jnp.stack(ks_rows), jnp.stack(vs_rows))
```

```python
import functools

import numpy as np
import jax
import jax.numpy as jnp
from jax import lax
from jax.experimental import pallas as pl
from jax.experimental.pallas import tpu as pltpu

F32 = jnp.float32
BF16 = jnp.bfloat16

N_MIXERS = 2
HEADS = 16
CHUNK = 64
ATTN_WINDOW = 8 * CHUNK
MAX_REL = 256
LN_EPS = 1e-5
RMS_EPS = 1e-6
LB_TINY = 1e-30
NEG_BIG = -1e30

LANES = 128
SUBLANES = 8
BF16_ROWS = 16
HGRN_CHUNK = 128
HGRN_HEADS_PROMPT = 16
HGRN_HEADS_SAMPLE = 16
HGRN_GROUP = 1
ATTN_TQ = ATTN_WINDOW
ATTN_SUB = 128
ATTN_KW = ATTN_SUB + ATTN_WINDOW
ATTN_HALF = 256
ATTN_HEADS = 16
OUT_TM = 512
OUT_SUB = 128
VMEM_LIMIT = 48 * 1024 * 1024
VMEM_LIMIT_BIG = 58 * 1024 * 1024
LOG2E = 1.4426950408889634

_DIMS_NT = (((1,), (1,)), ((), ()))


def _silu(x):
    h = 0.5 * x
    return h * jnp.tanh(h) + h


def _dot_nt(x, y):
    return lax.dot_general(x, y, _DIMS_NT, preferred_element_type=F32)


def _forget_lower_bound(lb_ref, layer_j):
    lbp = lb_ref[...].astype(F32)
    pe = jnp.exp(lbp - jnp.max(lbp, axis=0, keepdims=True))
    prob = pe / jnp.sum(pe, axis=0, keepdims=True)
    cs = prob[0:1]
    for i in range(1, layer_j + 1):
        cs = cs + prob[i:i + 1]
    return cs - prob[0:1]


def _mm_kernel(x_ref, w_ref, o_ref, w16_ref, *, silu):
    @pl.when(pl.program_id(1) == 0)
    def _():
        w16_ref[...] = w_ref[...].astype(BF16)

    acc = jnp.dot(x_ref[...], w16_ref[...], preferred_element_type=F32)
    o_ref[...] = (_silu(acc) if silu else acc).astype(o_ref.dtype)


def _in_proj(x16, w, layer, col0, n_cols, tm, tn, mode="plain", out_dtype=F32):
    M, K = x16.shape
    assert M % tm == 0 and n_cols % tn == 0 and col0 % tn == 0
    cb0 = col0 // tn
    return pl.pallas_call(
        functools.partial(_mm_kernel, silu=mode == "silu"),
        out_shape=jax.ShapeDtypeStruct((M, n_cols), out_dtype),
        grid=(n_cols // tn, M // tm),
        in_specs=[pl.BlockSpec((tm, K), lambda j, i: (i, 0)),
                  pl.BlockSpec((None, K, tn), lambda j, i: (layer, 0, cb0 + j))],
        out_specs=pl.BlockSpec((tm, tn), lambda j, i: (i, j)),
        scratch_shapes=[pltpu.VMEM((K, tn), BF16)],
        compiler_params=pltpu.CompilerParams(
            dimension_semantics=("arbitrary", "arbitrary"),
            vmem_limit_bytes=VMEM_LIMIT_BIG if tm > 1024 else VMEM_LIMIT),
        name="in_proj_" + mode,
    )(x16, w)


def _layer_norm_rows(y, g, b):
    mu = jnp.mean(y, axis=-1, keepdims=True)
    yc = y - mu
    var = jnp.mean(yc * yc, axis=-1, keepdims=True)
    return yc * lax.rsqrt(var + LN_EPS) * g + b


def _out_ln_kernel(m_ref, w_ref, h_ref, g_ref, b_ref, o_ref, o16_ref, *, alpha, sub):
    for r0 in range(0, m_ref.shape[0], sub):
        rows = slice(r0, r0 + sub)
        y = alpha * h_ref[rows] + jnp.dot(m_ref[rows], w_ref[...], preferred_element_type=F32)
        out = _layer_norm_rows(y, g_ref[...], b_ref[...])
        o_ref[rows] = out
        o16_ref[rows] = out.astype(BF16)


def _out_ln(mix16, w16, layer, h, g, b, alpha, tm):
    M, D = h.shape
    assert M % tm == 0
    row = lambda i: (i, 0)
    const = lambda i: (0, 0)
    return pl.pallas_call(
        functools.partial(_out_ln_kernel, alpha=alpha, sub=min(tm, OUT_SUB)),
        out_shape=(jax.ShapeDtypeStruct((M, D), F32), jax.ShapeDtypeStruct((M, D), BF16)),
        grid=(M // tm,),
        in_specs=[pl.BlockSpec((tm, D), row),
                  pl.BlockSpec((None, D, D), lambda i: (layer, 0, 0), pipeline_mode=pl.Buffered(1)),
                  pl.BlockSpec((tm, D), row), pl.BlockSpec((1, D), const), pl.BlockSpec((1, D), const)],
        out_specs=(pl.BlockSpec((tm, D), row), pl.BlockSpec((tm, D), row)),
        compiler_params=pltpu.CompilerParams(
            dimension_semantics=("arbitrary",), vmem_limit_bytes=VMEM_LIMIT),
        name="out_proj_ln",
    )(mix16, w16, h, g.reshape(1, D), b.reshape(1, D))


def _head(x, h):
    return x[:, h * LANES:(h + 1) * LANES]


def _hgrn_stage1(qs, z, lb, sub, C):
    W = qs.shape[1]
    heads = W // LANES
    pair = lambda x, y: [_dot_nt(_head(x, h), _head(y, h)) for h in range(heads)]
    lb_c = jnp.maximum(lb, LB_TINY)
    e = jnp.exp(-jnp.abs(z))
    pos = z >= 0
    inv = 1.0 / (1.0 + e)
    a = jnp.log2(jnp.where(pos, 1.0 + lb_c * e, e + lb_c) * inv)
    kk = (1.0 - lb) * (jnp.where(pos, e, 1.0) * inv)
    qs16 = qs.astype(BF16)
    kk16 = kk.astype(BF16)

    V = C // SUBLANES
    shape3 = (V, SUBLANES, W)
    p3 = a.reshape(shape3)
    qs3 = qs.reshape(shape3)
    kk3 = kk.reshape(shape3)
    parts = [pair(qs16, kk16)]
    for n in (1, 2, 4):
        if n == 1:
            bn = jnp.where((sub & 1) != 0, pltpu.roll(p3, 1, 1), p3)
        elif n == 2:
            bn = jnp.where(sub < 4, jnp.broadcast_to(p3[:, 1:2], shape3), jnp.broadcast_to(p3[:, 5:6], shape3))
        else:
            bn = jnp.broadcast_to(p3[:, 3:4], shape3)
        right = (sub & n) != 0
        ex = jnp.exp2(jnp.where(right, p3, bn - p3))
        qn = (qs3 * ex).reshape(C, W).astype(BF16)
        kn = (kk3 * ex).reshape(C, W).astype(BF16)
        parts.append(pair(qn, kn))
        p3 = p3 + jnp.where(right, bn, 0.0)
    p = p3.reshape(C, W)

    n = SUBLANES
    while n < C:
        q_rows, k_rows, p_rows = [], [], []
        for lo in range(0, C, 2 * n):
            mid, hi = lo + n, lo + 2 * n
            bn = p[mid - 1:mid]
            k_left = kk[lo:mid] * jnp.exp2(bn - p[lo:mid])
            q_right = qs[mid:hi] * jnp.exp2(p[mid:hi])
            if n >= BF16_ROWS:
                q_rows += [qs16[lo:mid], q_right.astype(BF16)]
                k_rows += [k_left.astype(BF16), kk16[mid:hi]]
            else:
                q_rows += [qs[lo:mid], q_right]
                k_rows += [k_left, kk[mid:hi]]
            p_rows += [p[lo:mid], p[mid:hi] + bn]
        parts.append(pair(jnp.concatenate(q_rows, axis=0).astype(BF16),
                          jnp.concatenate(k_rows, axis=0).astype(BF16)))
        p = jnp.concatenate(p_rows, axis=0)
        n *= 2

    b_end = p[C - 1:C, :]
    qe = (qs * jnp.exp2(p)).astype(BF16)
    kd = (kk * jnp.exp2(b_end - p)).astype(BF16)
    return parts, qe, kd, jnp.exp2(b_end)


def _hgrn_stage2(parts, qe, kd, dec, v16, scale, st_ref, store_mix, masks, C, h0):
    for i in range(qe.shape[1] // LANES):
        h = h0 + i
        amat = jnp.where(masks[0], parts[0][i], 0.0)
        for k in (1, 2, 3):
            amat = jnp.where(masks[k], parts[k][i], amat)
        n, k = SUBLANES, 4
        while n < C:
            rows = []
            for lo in range(0, C, 2 * n):
                mid, hi = lo + n, lo + 2 * n
                rows.append(amat[lo:mid])
                rows.append(jnp.where(masks[k][mid:hi], parts[k][i][mid:hi], amat[mid:hi]))
            amat = jnp.concatenate(rows, axis=0)
            n, k = 2 * n, k + 1

        st = st_ref[h]
        vh = _head(v16, i)
        o = jnp.dot(amat.astype(BF16), vh, preferred_element_type=F32) + _dot_nt(_head(qe, i), st.astype(BF16))
        st_ref[h] = st * _head(dec, i) + jnp.dot(vh.T, _head(kd, i), preferred_element_type=F32)
        ms = jnp.mean(o * o, axis=-1, keepdims=True)
        store_mix(h, o * lax.rsqrt(ms + RMS_EPS) * _head(scale, i))


def _hgrn_kernel(*refs, layer_j, chunk, heads, has_s0):
    if has_s0:
        lvl_ref, lb_ref, ng_ref, q_ref, z_ref, v_ref, g_ref, s0_ref, mix_ref, sfin_ref, st_ref = refs
    else:
        lvl_ref, lb_ref, ng_ref, q_ref, z_ref, v_ref, g_ref, mix_ref, sfin_ref, st_ref = refs
    C = chunk
    c = pl.program_id(2)

    @pl.when(c == 0)
    def _():
        for h in range(heads):
            if has_s0:
                st_ref[h] = s0_ref[0, h].T
            else:
                st_ref[h] = jnp.zeros((LANES, LANES), F32)

    ng = ng_ref[...].astype(F32)
    lvl = lvl_ref[...]
    n_levels = C.bit_length()
    masks = [lvl == k for k in range(n_levels)]
    group = min(HGRN_GROUP, heads)
    gw = group * LANES
    sub = lax.broadcasted_iota(jnp.int32, (1, SUBLANES, gw), 1)
    lb = _forget_lower_bound(lb_ref, layer_j)

    def store_mix(h, val):
        mix_ref[0, :, h * LANES:(h + 1) * LANES] = val.astype(mix_ref.dtype)

    pending = None
    for h0 in range(0, heads, group):
        cols = slice(h0 * LANES, h0 * LANES + gw)
        stage1 = _hgrn_stage1(q_ref[0, :, cols], z_ref[0, :, cols], lb[:, cols], sub, C)
        if pending is not None:
            _hgrn_stage2(*pending)
        pending = (*stage1, v_ref[0, :, cols], ng[:, cols] * g_ref[0, :, cols], st_ref, store_mix, masks, C, h0)
    _hgrn_stage2(*pending)

    @pl.when(c == pl.num_programs(2) - 1)
    def _():
        for h in range(heads):
            sfin_ref[0, h] = st_ref[h].T


def _level_ids(C):
    t = np.arange(C)[:, None]
    s = np.arange(C)[None, :]
    x = t ^ s
    lv = np.full((C, C), -1, np.int32)
    lv[x == 0] = 0
    n, k = 1, 1
    while n < C:
        lv[(x >= n) & (x < 2 * n) & (t > s)] = k
        n, k = 2 * n, k + 1
    return jnp.asarray(lv)


def _hgrn_mixer(qs, z, v16, gs, hgrn_lb, norm_g, layer_j, chunk, heads, s0=None):
    nb, T, D = qs.shape
    H = HEADS
    assert D // H == LANES and T % chunk == 0 and H % heads == 0 and chunk % BF16_ROWS == 0
    has_s0 = s0 is not None
    hg = H // heads
    w = heads * LANES
    tok_blk = pl.BlockSpec((1, chunk, w), lambda b, h, c: (b, c, h))
    st_blk = pl.BlockSpec((1, heads, LANES, LANES), lambda b, h, c: (b, h, 0, 0))
    in_specs = [
        pl.BlockSpec((chunk, chunk), lambda b, h, c: (0, 0)),
        pl.BlockSpec((hgrn_lb.shape[0], w), lambda b, h, c: (0, h)),
        pl.BlockSpec((1, w), lambda b, h, c: (0, h)),
        tok_blk, tok_blk, tok_blk, tok_blk,
    ]
    args = [_level_ids(chunk), hgrn_lb, norm_g.reshape(1, D), qs, z, v16, gs]
    if has_s0:
        in_specs.append(st_blk)
        args.append(s0)
    return pl.pallas_call(
        functools.partial(_hgrn_kernel, layer_j=layer_j, chunk=chunk, heads=heads, has_s0=has_s0),
        out_shape=(jax.ShapeDtypeStruct((nb, T, D), BF16),
                   jax.ShapeDtypeStruct((nb, H, LANES, LANES), F32)),
        grid=(nb, hg, T // chunk),
        in_specs=in_specs,
        out_specs=(pl.BlockSpec((1, chunk, w), lambda b, h, c: (b, c, h)), st_blk),
        scratch_shapes=[pltpu.VMEM((heads, LANES, LANES), F32)],
        compiler_params=pltpu.CompilerParams(
            dimension_semantics=("arbitrary", "arbitrary", "arbitrary"), vmem_limit_bytes=VMEM_LIMIT),
        name="hgrn_sample" if has_s0 else "hgrn_prompt",
    )(*args)


def _toeplitz_bias(table, rows, cols):
    H, n_rel = table.shape
    assert n_rel == MAX_REL + CHUNK
    d_hi = ATTN_WINDOW + rows - 1
    d_lo = ATTN_WINDOW - (cols - 1)
    left = d_hi - MAX_REL
    right = -(CHUNK - 1) - d_lo
    assert left >= 0 and right >= 0
    g = jnp.pad(table[:, ::-1], ((0, 0), (left, right)), mode="edge")
    n = rows + cols - 1
    g = jnp.pad(g, ((0, 0), (0, 1)))
    flat = jnp.tile(g, (1, rows))[:, rows - 1: rows - 1 + rows * n]
    return flat.reshape(H, rows, n)[:, :, :cols]


def _softmax2_pv(scores, values):
    m = functools.reduce(jnp.maximum, [jnp.max(s, axis=-1, keepdims=True) for s in scores])
    ps = [jnp.exp2(s - m) for s in scores]
    den = functools.reduce(lambda x, y: x + y, [jnp.sum(p, axis=-1, keepdims=True) for p in ps])
    o = functools.reduce(lambda x, y: x + y,
                         [jnp.dot(p.astype(BF16), v, preferred_element_type=F32) for p, v in zip(ps, values)])
    return o * (1.0 / den)


def _attn_prompt_kernel(bias_ref, q_ref, k_ref, v_ref, g_ref, mix_ref, kprev_ref, vprev_ref, *, qscale, heads):
    i = pl.program_id(2)
    tq = q_ref.shape[1]

    @pl.when(i == 0)
    def _():
        kprev_ref[...] = jnp.zeros_like(kprev_ref)
        vprev_ref[...] = jnp.zeros_like(vprev_ref)

    has_prev = i > 0

    halves = range(0, tq, ATTN_HALF)
    kwin = ATTN_HALF + ATTN_WINDOW

    def scores(h):
        sl = slice(h * LANES, (h + 1) * LANES)
        q16 = (q_ref[0, :, sl].astype(F32) * qscale).astype(BF16)
        kcat = jnp.concatenate([kprev_ref[:, sl], k_ref[0, :, sl].astype(BF16)], axis=0)
        return [_dot_nt(kcat[c0:c0 + kwin], q16[c0:c0 + ATTN_HALF]) for c0 in halves]

    s_next = scores(0)
    for h in range(heads):
        sl = slice(h * LANES, (h + 1) * LANES)
        s_halves = s_next
        if h + 1 < heads:
            s_next = scores(h + 1)
        bias = bias_ref[h]
        vcat_t = jnp.concatenate([vprev_ref[:, sl], v_ref[0, :, sl].astype(BF16)], axis=0).T
        o_cols = []
        for c0, s_half in zip(halves, s_halves):
            p_cols, inv_cols = [], []
            for r in range(0, ATTN_HALF, ATTN_SUB):
                n_prev = ATTN_WINDOW - c0 - r
                s = s_half[r:r + ATTN_KW, r:r + ATTN_SUB] + bias
                s = jnp.concatenate([jnp.where(has_prev, s[:n_prev], NEG_BIG), s[n_prev:]], axis=0)
                p = jnp.exp2(s - jnp.max(s, axis=0, keepdims=True))
                inv_cols.append(1.0 / jnp.sum(p, axis=0, keepdims=True))
                pieces = [p.astype(BF16)]
                if r:
                    pieces.insert(0, jnp.zeros((r, ATTN_SUB), BF16))
                if kwin - ATTN_KW - r:
                    pieces.append(jnp.zeros((kwin - ATTN_KW - r, ATTN_SUB), BF16))
                p_cols.append(jnp.concatenate(pieces, axis=0))
            o_half = jnp.dot(vcat_t[:, c0:c0 + kwin], jnp.concatenate(p_cols, axis=1), preferred_element_type=F32)
            o_cols.append(o_half * jnp.concatenate(inv_cols, axis=1))
        o = jnp.concatenate(o_cols, axis=1).T
        mix_ref[0, :, sl] = (o * _silu(g_ref[0, :, sl].astype(F32))).astype(mix_ref.dtype)

    kprev_ref[...] = k_ref[0].astype(BF16)
    vprev_ref[...] = v_ref[0].astype(BF16)


def _attn_prompt(proj, bias2):
    B, S, D4 = proj.shape
    D = D4 // 4
    H = HEADS
    tq = ATTN_TQ
    heads = ATTN_HEADS
    hg = H // heads
    w = heads * LANES
    assert D // H == LANES and S % tq == 0 and H % heads == 0
    col = lambda off: (lambda b, h, i: (b, i, off * hg + h))
    blk = (1, tq, w)
    return pl.pallas_call(
        functools.partial(_attn_prompt_kernel, qscale=float(LANES) ** -0.5 * LOG2E, heads=heads),
        out_shape=jax.ShapeDtypeStruct((B, S, D), BF16),
        grid=(B, hg, S // tq),
        in_specs=[pl.BlockSpec((heads, ATTN_KW, ATTN_SUB), lambda b, h, i: (h, 0, 0)),
                  pl.BlockSpec(blk, col(0)), pl.BlockSpec(blk, col(1)),
                  pl.BlockSpec(blk, col(2)), pl.BlockSpec(blk, col(3))],
        out_specs=pl.BlockSpec(blk, lambda b, h, i: (b, i, h)),
        scratch_shapes=[pltpu.VMEM((tq, w), BF16), pltpu.VMEM((tq, w), BF16)],
        compiler_params=pltpu.CompilerParams(
            dimension_semantics=("arbitrary", "arbitrary", "arbitrary"), vmem_limit_bytes=VMEM_LIMIT),
        name="attn_prompt",
    )(bias2, proj, proj, proj, proj)


def _attn_sample_kernel(bc_ref, bn_ref, q_ref, kn_ref, vn_ref, g_ref, kc_ref, vc_ref, mix_ref, *, qscale, heads):
    win = kc_ref.shape[1] // heads
    for h in range(heads):
        sl = slice(h * LANES, (h + 1) * LANES)
        kc = kc_ref[0, pl.ds(h, win, stride=heads), :].astype(BF16)
        vc = vc_ref[0, pl.ds(h, win, stride=heads), :].astype(BF16)
        q16 = (q_ref[0, :, sl].astype(F32) * qscale).astype(BF16)
        s_c = _dot_nt(q16, kc) + bc_ref[h]
        s_n = _dot_nt(q16, kn_ref[0, :, sl].astype(BF16)) + bn_ref[h]
        o = _softmax2_pv([s_c, s_n], [vc, vn_ref[0, :, sl].astype(BF16)])
        mix_ref[0, :, sl] = (o * _silu(g_ref[0, :, sl].astype(F32))).astype(mix_ref.dtype)


def _attn_sample(proj, cache_k, cache_v, layer_j, bias_c2, bias_n2):
    nb, T, D4 = proj.shape
    D = D4 // 4
    H = HEADS
    W = cache_k.shape[2] // H
    new = lambda off: (lambda b: (b, 0, off))
    blk = (1, T, D)
    cblk = (None, 1, W * H, LANES)
    return pl.pallas_call(
        functools.partial(_attn_sample_kernel, qscale=float(LANES) ** -0.5 * LOG2E, heads=H),
        out_shape=jax.ShapeDtypeStruct((nb, T, D), BF16),
        grid=(nb,),
        in_specs=[pl.BlockSpec((H, T, W), lambda b: (0, 0, 0)),
                  pl.BlockSpec((H, T, T), lambda b: (0, 0, 0)),
                  pl.BlockSpec(blk, new(0)), pl.BlockSpec(blk, new(1)),
                  pl.BlockSpec(blk, new(2)), pl.BlockSpec(blk, new(3)),
                  pl.BlockSpec(cblk, lambda b: (layer_j, b, 0, 0)),
                  pl.BlockSpec(cblk, lambda b: (layer_j, b, 0, 0))],
        out_specs=pl.BlockSpec(blk, lambda b: (b, 0, 0)),
        compiler_params=pltpu.CompilerParams(
            dimension_semantics=("arbitrary",), vmem_limit_bytes=VMEM_LIMIT),
        name="attn_sample",
    )(bias_c2, bias_n2, proj, proj, proj, proj, cache_k, cache_v)


def _row_tile(m, cap):
    t = cap
    while m % t:
        t //= 2
    return t


def kernel(x_prompt, x_sample, state_hgrn, cache_attn_k, cache_attn_v, w_in, w_out, ln_g, ln_b,
           hgrn_lb, hgrn_norm_g, attn_rel_bias):
    B, S, D = x_prompt.shape
    nb, T, _ = x_sample.shape
    depth = w_in.shape[0]
    H = HEADS
    hd = D // H
    alpha = (2 * depth) ** 0.25
    kv_win = cache_attn_k.shape[2]
    prompt_win = min(ATTN_WINDOW, S)
    assert kv_win == ATTN_WINDOW

    w_out16 = w_out.astype(BF16)
    hp = x_prompt.reshape(B * S, D)
    hs = x_sample.reshape(nb * T, D)
    hp16 = hp.astype(BF16)
    hs16 = hs.astype(BF16)
    tm_p = _row_tile(B * S, 1024)
    tm_s = _row_tile(nb * T, 512)
    tn = 1024
    cache_k = cache_attn_k.reshape(cache_attn_k.shape[0], nb, kv_win * H, hd)
    cache_v = cache_attn_v.reshape(cache_attn_v.shape[0], nb, kv_win * H, hd)

    qc = (np.arange(ATTN_SUB)[:, None] + ATTN_WINDOW) // CHUNK
    kc = np.arange(ATTN_KW)[None, :] // CHUNK
    band_ok = jnp.asarray((kc <= qc) & (kc >= qc - ATTN_WINDOW // CHUNK))

    st_p, st_s, kp_rows, vp_rows, ks_rows, vs_rows = [], [], [], [], [], []
    for layer in range(depth):
        j = layer // N_MIXERS
        if layer % N_MIXERS == 0:
            def sections(x16, nseq, tm):
                sec = lambda y: y.reshape(nseq, -1, D)
                return (sec(_in_proj(x16, w_in, layer, 0, D, tm, tn, "silu")),
                        sec(_in_proj(x16, w_in, layer, D, D, tm, tn, "plain")),
                        sec(_in_proj(x16, w_in, layer, 2 * D, D, tm, tn, "plain", BF16)),
                        sec(_in_proj(x16, w_in, layer, 3 * D, D, tm, tn, "silu")))

            mix_p, s_fin = _hgrn_mixer(*sections(hp16, B, tm_p), hgrn_lb, hgrn_norm_g[j], j, min(HGRN_CHUNK, S),
                                       HGRN_HEADS_PROMPT)
            mix_s, s_new = _hgrn_mixer(*sections(hs16, nb, tm_s), hgrn_lb, hgrn_norm_g[j], j, T,
                                       HGRN_HEADS_SAMPLE, s0=state_hgrn[j])
            st_p.append(s_fin.astype(state_hgrn.dtype))
            st_s.append(s_new.astype(state_hgrn.dtype))
        else:
            proj_p = _in_proj(hp16, w_in, layer, 0, 4 * D, _row_tile(B * S, 2048), tn, "plain",
                              BF16).reshape(B, S, 4 * D)
            proj_s = _in_proj(hs16, w_in, layer, 0, 4 * D, tm_s, tn, "plain", BF16).reshape(nb, T, 4 * D)
            toep2 = _toeplitz_bias(attn_rel_bias[j].astype(F32), ATTN_SUB, ATTN_KW) * LOG2E
            mix_p = _attn_prompt(proj_p, jnp.swapaxes(jnp.where(band_ok, toep2, NEG_BIG), 1, 2))
            mix_s = _attn_sample(proj_s, cache_k, cache_v, j,
                                 toep2[:, :T, :kv_win], toep2[:, :T, kv_win:kv_win + T])
            out_dt = x_prompt.dtype
            kp_rows.append(proj_p[:, S - prompt_win:, D:2 * D].astype(out_dt).reshape(B, prompt_win, H, hd))
            vp_rows.append(proj_p[:, S - prompt_win:, 2 * D:3 * D].astype(out_dt).reshape(B, prompt_win, H, hd))
            ks_rows.append(proj_s[:, :, D:2 * D].astype(out_dt).reshape(nb, T, H, hd))
            vs_rows.append(proj_s[:, :, 2 * D:3 * D].astype(out_dt).reshape(nb, T, H, hd))
        hp, hp16 = _out_ln(mix_p.reshape(B * S, D), w_out16, layer, hp, ln_g[layer], ln_b[layer], alpha,
                           _row_tile(B * S, OUT_TM))
        hs, hs16 = _out_ln(mix_s.reshape(nb * T, D), w_out16, layer, hs, ln_g[layer], ln_b[layer], alpha,
                           _row_tile(nb * T, OUT_TM))
    return (hp.reshape(B, S, D), hs.reshape(nb, T, D),
            jnp.stack(st_p), jnp.stack(st_s),
            jnp.stack(kp_rows), jnp.stack(vp_rows), jnp.stack(ks_rows), jnp.stack(vs_rows))
```

```python
import functools

import numpy as np
import jax
import jax.numpy as jnp
from jax import lax
from jax.experimental import pallas as pl
from jax.experimental.pallas import tpu as pltpu

F32 = jnp.float32
BF16 = jnp.bfloat16

N_MIXERS = 2
HEADS = 16
CHUNK = 64
ATTN_WINDOW = 8 * CHUNK
MAX_REL = 256
LN_EPS = 1e-5
RMS_EPS = 1e-6
LB_TINY = 1e-30
NEG_BIG = -1e30

LANES = 128
SUBLANES = 8
BF16_ROWS = 16
HGRN_CHUNK = 128
HGRN_HEADS_PROMPT = 16
HGRN_HEADS_SAMPLE = 16
HGRN_GROUP = 1
ATTN_TQ = ATTN_WINDOW
ATTN_SUB = 128
ATTN_KW = ATTN_SUB + ATTN_WINDOW
ATTN_HALF = 256
ATTN_HEADS = 16
OUT_TM = 512
OUT_SUB = 128
VMEM_LIMIT = 48 * 1024 * 1024
VMEM_LIMIT_BIG = 58 * 1024 * 1024
LOG2E = 1.4426950408889634

_DIMS_NT = (((1,), (1,)), ((), ()))


def _silu(x):
    h = 0.5 * x
    return h * jnp.tanh(h) + h


def _dot_nt(x, y):
    return lax.dot_general(x, y, _DIMS_NT, preferred_element_type=F32)


def _forget_lower_bound(lb_ref, layer_j):
    lbp = lb_ref[...].astype(F32)
    pe = jnp.exp(lbp - jnp.max(lbp, axis=0, keepdims=True))
    prob = pe / jnp.sum(pe, axis=0, keepdims=True)
    cs = prob[0:1]
    for i in range(1, layer_j + 1):
        cs = cs + prob[i:i + 1]
    return cs - prob[0:1]


def _mm_kernel(x_ref, w_ref, o_ref, w16_ref, *, silu):
    @pl.when(pl.program_id(1) == 0)
    def _():
        w16_ref[...] = w_ref[...].astype(BF16)

    acc = jnp.dot(x_ref[...].astype(BF16), w16_ref[...], preferred_element_type=F32)
    o_ref[...] = (_silu(acc) if silu else acc).astype(o_ref.dtype)


def _in_proj(x16, w, layer, col0, n_cols, tm, tn, mode="plain", out_dtype=F32):
    M, K = x16.shape
    assert M % tm == 0 and n_cols % tn == 0 and col0 % tn == 0
    cb0 = col0 // tn
    return pl.pallas_call(
        functools.partial(_mm_kernel, silu=mode == "silu"),
        out_shape=jax.ShapeDtypeStruct((M, n_cols), out_dtype),
        grid=(n_cols // tn, M // tm),
        in_specs=[pl.BlockSpec((tm, K), lambda j, i: (i, 0)),
                  pl.BlockSpec((None, K, tn), lambda j, i: (layer, 0, cb0 + j))],
        out_specs=pl.BlockSpec((tm, tn), lambda j, i: (i, j)),
        scratch_shapes=[pltpu.VMEM((K, tn), BF16)],
        compiler_params=pltpu.CompilerParams(
            dimension_semantics=("arbitrary", "arbitrary"),
            vmem_limit_bytes=VMEM_LIMIT_BIG if tm > 1024 else VMEM_LIMIT),
        name="in_proj_" + mode,
    )(x16, w)


def _layer_norm_rows(y, g, b):
    mu = jnp.mean(y, axis=-1, keepdims=True)
    yc = y - mu
    var = jnp.mean(yc * yc, axis=-1, keepdims=True)
    return yc * lax.rsqrt(var + LN_EPS) * g + b


def _out_ln_kernel(m_ref, w_ref, h_ref, g_ref, b_ref, o_ref, o16_ref, *, alpha, sub):
    for r0 in range(0, m_ref.shape[0], sub):
        rows = slice(r0, r0 + sub)
        y = alpha * h_ref[rows] + jnp.dot(m_ref[rows], w_ref[...], preferred_element_type=F32)
        out = _layer_norm_rows(y, g_ref[...], b_ref[...])
        o_ref[rows] = out
        o16_ref[rows] = out.astype(BF16)


def _out_ln(mix16, w16, layer, h, g, b, alpha, tm):
    M, D = h.shape
    assert M % tm == 0
    row = lambda i: (i, 0)
    const = lambda i: (0, 0)
    return pl.pallas_call(
        functools.partial(_out_ln_kernel, alpha=alpha, sub=min(tm, OUT_SUB)),
        out_shape=(jax.ShapeDtypeStruct((M, D), F32), jax.ShapeDtypeStruct((M, D), BF16)),
        grid=(M // tm,),
        in_specs=[pl.BlockSpec((tm, D), row),
                  pl.BlockSpec((None, D, D), lambda i: (layer, 0, 0), pipeline_mode=pl.Buffered(1)),
                  pl.BlockSpec((tm, D), row), pl.BlockSpec((1, D), const), pl.BlockSpec((1, D), const)],
        out_specs=(pl.BlockSpec((tm, D), row), pl.BlockSpec((tm, D), row)),
        compiler_params=pltpu.CompilerParams(
            dimension_semantics=("arbitrary",), vmem_limit_bytes=VMEM_LIMIT),
        name="out_proj_ln",
    )(mix16, w16, h, g.reshape(1, D), b.reshape(1, D))


def _head(x, h):
    return x[:, h * LANES:(h + 1) * LANES]


def _hgrn_stage1(qs, z, lb, sub, C):
    W = qs.shape[1]
    heads = W // LANES
    pair = lambda x, y: [_dot_nt(_head(x, h), _head(y, h)) for h in range(heads)]
    lb_c = jnp.maximum(lb, LB_TINY)
    e = jnp.exp(-jnp.abs(z))
    pos = z >= 0
    inv = 1.0 / (1.0 + e)
    a = jnp.log2(jnp.where(pos, 1.0 + lb_c * e, e + lb_c) * inv)
    kk = (1.0 - lb) * (jnp.where(pos, e, 1.0) * inv)
    qs16 = qs.astype(BF16)
    kk16 = kk.astype(BF16)

    V = C // SUBLANES
    shape3 = (V, SUBLANES, W)
    p3 = a.reshape(shape3)
    qs3 = qs.reshape(shape3)
    kk3 = kk.reshape(shape3)
    parts = [pair(qs16, kk16)]
    for n in (1, 2, 4):
        if n == 1:
            bn = jnp.where((sub & 1) != 0, pltpu.roll(p3, 1, 1), p3)
        elif n == 2:
            bn = jnp.where(sub < 4, jnp.broadcast_to(p3[:, 1:2], shape3), jnp.broadcast_to(p3[:, 5:6], shape3))
        else:
            bn = jnp.broadcast_to(p3[:, 3:4], shape3)
        right = (sub & n) != 0
        ex = jnp.exp2(jnp.where(right, p3, bn - p3))
        qn = (qs3 * ex).reshape(C, W).astype(BF16)
        kn = (kk3 * ex).reshape(C, W).astype(BF16)
        parts.append(pair(qn, kn))
        p3 = p3 + jnp.where(right, bn, 0.0)
    p = p3.reshape(C, W)

    n = SUBLANES
    while n < C:
        q_rows, k_rows, p_rows = [], [], []
        for lo in range(0, C, 2 * n):
            mid, hi = lo + n, lo + 2 * n
            bn = p[mid - 1:mid]
            k_left = kk[lo:mid] * jnp.exp2(bn - p[lo:mid])
            q_right = qs[mid:hi] * jnp.exp2(p[mid:hi])
            if n >= BF16_ROWS:
                q_rows += [qs16[lo:mid], q_right.astype(BF16)]
                k_rows += [k_left.astype(BF16), kk16[mid:hi]]
            else:
                q_rows += [qs[lo:mid], q_right]
                k_rows += [k_left, kk[mid:hi]]
            p_rows += [p[lo:mid], p[mid:hi] + bn]
        parts.append(pair(jnp.concatenate(q_rows, axis=0).astype(BF16),
                          jnp.concatenate(k_rows, axis=0).astype(BF16)))
        p = jnp.concatenate(p_rows, axis=0)
        n *= 2

    b_end = p[C - 1:C, :]
    qe = (qs * jnp.exp2(p)).astype(BF16)
    kd = (kk * jnp.exp2(b_end - p)).astype(BF16)
    return parts, qe, kd, jnp.exp2(b_end)


def _hgrn_stage2(parts, qe, kd, dec, v16, scale, st_ref, store_mix, masks, C, h0):
    for i in range(qe.shape[1] // LANES):
        h = h0 + i
        amat = jnp.where(masks[0], parts[0][i], 0.0)
        for k in (1, 2, 3):
            amat = jnp.where(masks[k], parts[k][i], amat)
        n, k = SUBLANES, 4
        while n < C:
            rows = []
            for lo in range(0, C, 2 * n):
                mid, hi = lo + n, lo + 2 * n
                rows.append(amat[lo:mid])
                rows.append(jnp.where(masks[k][mid:hi], parts[k][i][mid:hi], amat[mid:hi]))
            amat = jnp.concatenate(rows, axis=0)
            n, k = 2 * n, k + 1

        st = st_ref[h]
        vh = _head(v16, i)
        o = jnp.dot(amat.astype(BF16), vh, preferred_element_type=F32) + _dot_nt(_head(qe, i), st.astype(BF16))
        st_ref[h] = st * _head(dec, i) + jnp.dot(vh.T, _head(kd, i), preferred_element_type=F32)
        ms = jnp.mean(o * o, axis=-1, keepdims=True)
        store_mix(h, o * lax.rsqrt(ms + RMS_EPS) * _head(scale, i))


def _hgrn_kernel(*refs, layer_j, chunk, heads, has_s0):
    if has_s0:
        lvl_ref, lb_ref, ng_ref, q_ref, z_ref, v_ref, g_ref, s0_ref, mix_ref, sfin_ref, st_ref = refs
    else:
        lvl_ref, lb_ref, ng_ref, q_ref, z_ref, v_ref, g_ref, mix_ref, sfin_ref, st_ref = refs
    C = chunk
    c = pl.program_id(2)

    @pl.when(c == 0)
    def _():
        for h in range(heads):
            if has_s0:
                st_ref[h] = s0_ref[0, h].T
            else:
                st_ref[h] = jnp.zeros((LANES, LANES), F32)

    ng = ng_ref[...].astype(F32)
    lvl = lvl_ref[...]
    n_levels = C.bit_length()
    masks = [lvl == k for k in range(n_levels)]
    group = min(HGRN_GROUP, heads)
    gw = group * LANES
    sub = lax.broadcasted_iota(jnp.int32, (1, SUBLANES, gw), 1)
    lb = _forget_lower_bound(lb_ref, layer_j)

    def store_mix(h, val):
        mix_ref[0, :, h * LANES:(h + 1) * LANES] = val.astype(mix_ref.dtype)

    pending = None
    for h0 in range(0, heads, group):
        cols = slice(h0 * LANES, h0 * LANES + gw)
        stage1 = _hgrn_stage1(q_ref[0, :, cols], z_ref[0, :, cols], lb[:, cols], sub, C)
        if pending is not None:
            _hgrn_stage2(*pending)
        pending = (*stage1, v_ref[0, :, cols], ng[:, cols] * g_ref[0, :, cols], st_ref, store_mix, masks, C, h0)
    _hgrn_stage2(*pending)

    @pl.when(c == pl.num_programs(2) - 1)
    def _():
        for h in range(heads):
            sfin_ref[0, h] = st_ref[h].T


def _level_ids(C):
    t = np.arange(C)[:, None]
    s = np.arange(C)[None, :]
    x = t ^ s
    lv = np.full((C, C), -1, np.int32)
    lv[x == 0] = 0
    n, k = 1, 1
    while n < C:
        lv[(x >= n) & (x < 2 * n) & (t > s)] = k
        n, k = 2 * n, k + 1
    return jnp.asarray(lv)


def _hgrn_mixer(qs, z, v16, gs, hgrn_lb, norm_g, layer_j, chunk, heads, s0=None):
    nb, T, D = qs.shape
    H = HEADS
    assert D // H == LANES and T % chunk == 0 and H % heads == 0 and chunk % BF16_ROWS == 0
    has_s0 = s0 is not None
    hg = H // heads
    w = heads * LANES
    tok_blk = pl.BlockSpec((1, chunk, w), lambda b, h, c: (b, c, h))
    st_blk = pl.BlockSpec((1, heads, LANES, LANES), lambda b, h, c: (b, h, 0, 0))
    in_specs = [
        pl.BlockSpec((chunk, chunk), lambda b, h, c: (0, 0)),
        pl.BlockSpec((hgrn_lb.shape[0], w), lambda b, h, c: (0, h)),
        pl.BlockSpec((1, w), lambda b, h, c: (0, h)),
        tok_blk, tok_blk, tok_blk, tok_blk,
    ]
    args = [_level_ids(chunk), hgrn_lb, norm_g.reshape(1, D), qs, z, v16, gs]
    if has_s0:
        in_specs.append(st_blk)
        args.append(s0)
    return pl.pallas_call(
        functools.partial(_hgrn_kernel, layer_j=layer_j, chunk=chunk, heads=heads, has_s0=has_s0),
        out_shape=(jax.ShapeDtypeStruct((nb, T, D), BF16),
                   jax.ShapeDtypeStruct((nb, H, LANES, LANES), F32)),
        grid=(nb, hg, T // chunk),
        in_specs=in_specs,
        out_specs=(pl.BlockSpec((1, chunk, w), lambda b, h, c: (b, c, h)), st_blk),
        scratch_shapes=[pltpu.VMEM((heads, LANES, LANES), F32)],
        compiler_params=pltpu.CompilerParams(
            dimension_semantics=("arbitrary", "arbitrary", "arbitrary"), vmem_limit_bytes=VMEM_LIMIT),
        name="hgrn_sample" if has_s0 else "hgrn_prompt",
    )(*args)


def _toeplitz_bias(table, rows, cols):
    H, n_rel = table.shape
    assert n_rel == MAX_REL + CHUNK
    d_hi = ATTN_WINDOW + rows - 1
    d_lo = ATTN_WINDOW - (cols - 1)
    left = d_hi - MAX_REL
    right = -(CHUNK - 1) - d_lo
    assert left >= 0 and right >= 0
    g = jnp.pad(table[:, ::-1], ((0, 0), (left, right)), mode="edge")
    n = rows + cols - 1
    g = jnp.pad(g, ((0, 0), (0, 1)))
    flat = jnp.tile(g, (1, rows))[:, rows - 1: rows - 1 + rows * n]
    return flat.reshape(H, rows, n)[:, :, :cols]


def _softmax2_pv(scores, values):
    m = functools.reduce(jnp.maximum, [jnp.max(s, axis=-1, keepdims=True) for s in scores])
    ps = [jnp.exp2(s - m) for s in scores]
    den = functools.reduce(lambda x, y: x + y, [jnp.sum(p, axis=-1, keepdims=True) for p in ps])
    o = functools.reduce(lambda x, y: x + y,
                         [jnp.dot(p.astype(BF16), v, preferred_element_type=F32) for p, v in zip(ps, values)])
    return o * (1.0 / den)


def _attn_prompt_kernel(bias_ref, q_ref, k_ref, v_ref, g_ref, mix_ref, kprev_ref, vprev_ref, *, qscale, heads):
    i = pl.program_id(2)
    tq = q_ref.shape[1]

    @pl.when(i == 0)
    def _():
        kprev_ref[...] = jnp.zeros_like(kprev_ref)
        vprev_ref[...] = jnp.zeros_like(vprev_ref)

    has_prev = i > 0

    halves = range(0, tq, ATTN_HALF)
    kwin = ATTN_HALF + ATTN_WINDOW

    def scores(h):
        sl = slice(h * LANES, (h + 1) * LANES)
        q16 = (q_ref[0, :, sl].astype(F32) * qscale).astype(BF16)
        kcat = jnp.concatenate([kprev_ref[:, sl], k_ref[0, :, sl].astype(BF16)], axis=0)
        return [_dot_nt(kcat[c0:c0 + kwin], q16[c0:c0 + ATTN_HALF]) for c0 in halves]

    s_next = scores(0)
    for h in range(heads):
        sl = slice(h * LANES, (h + 1) * LANES)
        s_halves = s_next
        if h + 1 < heads:
            s_next = scores(h + 1)
        bias = bias_ref[h]
        vcat_t = jnp.concatenate([vprev_ref[:, sl], v_ref[0, :, sl].astype(BF16)], axis=0).T
        o_cols = []
        for c0, s_half in zip(halves, s_halves):
            p_cols, inv_cols = [], []
            for r in range(0, ATTN_HALF, ATTN_SUB):
                n_prev = ATTN_WINDOW - c0 - r
                s = s_half[r:r + ATTN_KW, r:r + ATTN_SUB] + bias
                s = jnp.concatenate([jnp.where(has_prev, s[:n_prev], NEG_BIG), s[n_prev:]], axis=0)
                p = jnp.exp2(s - jnp.max(s, axis=0, keepdims=True))
                inv_cols.append(1.0 / jnp.sum(p, axis=0, keepdims=True))
                pieces = [p.astype(BF16)]
                if r:
                    pieces.insert(0, jnp.zeros((r, ATTN_SUB), BF16))
                if kwin - ATTN_KW - r:
                    pieces.append(jnp.zeros((kwin - ATTN_KW - r, ATTN_SUB), BF16))
                p_cols.append(jnp.concatenate(pieces, axis=0))
            o_half = jnp.dot(vcat_t[:, c0:c0 + kwin], jnp.concatenate(p_cols, axis=1), preferred_element_type=F32)
            o_cols.append(o_half * jnp.concatenate(inv_cols, axis=1))
        o = jnp.concatenate(o_cols, axis=1).T
        mix_ref[0, :, sl] = (o * _silu(g_ref[0, :, sl].astype(F32))).astype(mix_ref.dtype)

    kprev_ref[...] = k_ref[0].astype(BF16)
    vprev_ref[...] = v_ref[0].astype(BF16)


def _attn_prompt(proj, bias2):
    B, S, D4 = proj.shape
    D = D4 // 4
    H = HEADS
    tq = ATTN_TQ
    heads = ATTN_HEADS
    hg = H // heads
    w = heads * LANES
    assert D // H == LANES and S % tq == 0 and H % heads == 0
    col = lambda off: (lambda b, h, i: (b, i, off * hg + h))
    blk = (1, tq, w)
    return pl.pallas_call(
        functools.partial(_attn_prompt_kernel, qscale=float(LANES) ** -0.5 * LOG2E, heads=heads),
        out_shape=jax.ShapeDtypeStruct((B, S, D), BF16),
        grid=(B, hg, S // tq),
        in_specs=[pl.BlockSpec((heads, ATTN_KW, ATTN_SUB), lambda b, h, i: (h, 0, 0)),
                  pl.BlockSpec(blk, col(0)), pl.BlockSpec(blk, col(1)),
                  pl.BlockSpec(blk, col(2)), pl.BlockSpec(blk, col(3))],
        out_specs=pl.BlockSpec(blk, lambda b, h, i: (b, i, h)),
        scratch_shapes=[pltpu.VMEM((tq, w), BF16), pltpu.VMEM((tq, w), BF16)],
        compiler_params=pltpu.CompilerParams(
            dimension_semantics=("arbitrary", "arbitrary", "arbitrary"), vmem_limit_bytes=VMEM_LIMIT),
        name="attn_prompt",
    )(bias2, proj, proj, proj, proj)


def _attn_sample_kernel(bc_ref, bn_ref, q_ref, kn_ref, vn_ref, g_ref, kc_ref, vc_ref, mix_ref, *, qscale, heads):
    win = kc_ref.shape[1] // heads
    for h in range(heads):
        sl = slice(h * LANES, (h + 1) * LANES)
        kc = kc_ref[0, pl.ds(h, win, stride=heads), :].astype(BF16)
        vc = vc_ref[0, pl.ds(h, win, stride=heads), :].astype(BF16)
        q16 = (q_ref[0, :, sl].astype(F32) * qscale).astype(BF16)
        s_c = _dot_nt(q16, kc) + bc_ref[h]
        s_n = _dot_nt(q16, kn_ref[0, :, sl].astype(BF16)) + bn_ref[h]
        o = _softmax2_pv([s_c, s_n], [vc, vn_ref[0, :, sl].astype(BF16)])
        mix_ref[0, :, sl] = (o * _silu(g_ref[0, :, sl].astype(F32))).astype(mix_ref.dtype)


def _attn_sample(proj, cache_k, cache_v, layer_j, bias_c2, bias_n2):
    nb, T, D4 = proj.shape
    D = D4 // 4
    H = HEADS
    W = cache_k.shape[2] // H
    new = lambda off: (lambda b: (b, 0, off))
    blk = (1, T, D)
    cblk = (None, 1, W * H, LANES)
    return pl.pallas_call(
        functools.partial(_attn_sample_kernel, qscale=float(LANES) ** -0.5 * LOG2E, heads=H),
        out_shape=jax.ShapeDtypeStruct((nb, T, D), BF16),
        grid=(nb,),
        in_specs=[pl.BlockSpec((H, T, W), lambda b: (0, 0, 0)),
                  pl.BlockSpec((H, T, T), lambda b: (0, 0, 0)),
                  pl.BlockSpec(blk, new(0)), pl.BlockSpec(blk, new(1)),
                  pl.BlockSpec(blk, new(2)), pl.BlockSpec(blk, new(3)),
                  pl.BlockSpec(cblk, lambda b: (layer_j, b, 0, 0)),
                  pl.BlockSpec(cblk, lambda b: (layer_j, b, 0, 0))],
        out_specs=pl.BlockSpec(blk, lambda b: (b, 0, 0)),
        compiler_params=pltpu.CompilerParams(
            dimension_semantics=("arbitrary",), vmem_limit_bytes=VMEM_LIMIT),
        name="attn_sample",
    )(bias_c2, bias_n2, proj, proj, proj, proj, cache_k, cache_v)


def _row_tile(m, cap):
    t = cap
    while m % t:
        t //= 2
    return t


def kernel(x_prompt, x_sample, state_hgrn, cache_attn_k, cache_attn_v, w_in, w_out, ln_g, ln_b,
           hgrn_lb, hgrn_norm_g, attn_rel_bias):
    B, S, D = x_prompt.shape
    nb, T, _ = x_sample.shape
    depth = w_in.shape[0]
    H = HEADS
    hd = D // H
    alpha = (2 * depth) ** 0.25
    kv_win = cache_attn_k.shape[2]
    prompt_win = min(ATTN_WINDOW, S)
    assert kv_win == ATTN_WINDOW

    w_out16 = w_out.astype(BF16)
    hp = x_prompt.reshape(B * S, D)
    hs = x_sample.reshape(nb * T, D)
    hp16, hs16 = hp, hs
    tm_p = _row_tile(B * S, 1024)
    tm_s = _row_tile(nb * T, 512)
    tn = 1024
    cache_k = cache_attn_k.reshape(cache_attn_k.shape[0], nb, kv_win * H, hd)
    cache_v = cache_attn_v.reshape(cache_attn_v.shape[0], nb, kv_win * H, hd)

    qc = (np.arange(ATTN_SUB)[:, None] + ATTN_WINDOW) // CHUNK
    kc = np.arange(ATTN_KW)[None, :] // CHUNK
    band_ok = jnp.asarray((kc <= qc) & (kc >= qc - ATTN_WINDOW // CHUNK))

    st_p, st_s, kp_rows, vp_rows, ks_rows, vs_rows = [], [], [], [], [], []
    for layer in range(depth):
        j = layer // N_MIXERS
        if layer % N_MIXERS == 0:
            def sections(x16, nseq, tm):
                sec = lambda y: y.reshape(nseq, -1, D)
                return (sec(_in_proj(x16, w_in, layer, 0, D, tm, tn, "silu")),
                        sec(_in_proj(x16, w_in, layer, D, D, tm, tn, "plain")),
                        sec(_in_proj(x16, w_in, layer, 2 * D, D, tm, tn, "plain", BF16)),
                        sec(_in_proj(x16, w_in, layer, 3 * D, D, tm, tn, "silu")))

            mix_p, s_fin = _hgrn_mixer(*sections(hp16, B, tm_p), hgrn_lb, hgrn_norm_g[j], j, min(HGRN_CHUNK, S),
                                       HGRN_HEADS_PROMPT)
            mix_s, s_new = _hgrn_mixer(*sections(hs16, nb, tm_s), hgrn_lb, hgrn_norm_g[j], j, T,
                                       HGRN_HEADS_SAMPLE, s0=state_hgrn[j])
            st_p.append(s_fin.astype(state_hgrn.dtype))
            st_s.append(s_new.astype(state_hgrn.dtype))
        else:
            proj_p = _in_proj(hp16, w_in, layer, 0, 4 * D, _row_tile(B * S, 2048), tn, "plain",
                              BF16).reshape(B, S, 4 * D)
            proj_s = _in_proj(hs16, w_in, layer, 0, 4 * D, tm_s, tn, "plain", BF16).reshape(nb, T, 4 * D)
            toep2 = _toeplitz_bias(attn_rel_bias[j].astype(F32), ATTN_SUB, ATTN_KW) * LOG2E
            mix_p = _attn_prompt(proj_p, jnp.swapaxes(jnp.where(band_ok, toep2, NEG_BIG), 1, 2))
            mix_s = _attn_sample(proj_s, cache_k, cache_v, j,
                                 toep2[:, :T, :kv_win], toep2[:, :T, kv_win:kv_win + T])
            out_dt = x_prompt.dtype
            kp_rows.append(proj_p[:, S - prompt_win:, D:2 * D].astype(out_dt).reshape(B, prompt_win, H, hd))
            vp_rows.append(proj_p[:, S - prompt_win:, 2 * D:3 * D].astype(out_dt).reshape(B, prompt_win, H, hd))
            ks_rows.append(proj_s[:, :, D:2 * D].astype(out_dt).reshape(nb, T, H, hd))
            vs_rows.append(proj_s[:, :, 2 * D:3 * D].astype(out_dt).reshape(nb, T, H, hd))
        hp, hp16 = _out_ln(mix_p.reshape(B * S, D), w_out16, layer, hp, ln_g[layer], ln_b[layer], alpha,
                           _row_tile(B * S, OUT_TM))
        hs, hs16 = _out_ln(mix_s.reshape(nb * T, D), w_out16, layer, hs, ln_g[layer], ln_b[layer], alpha,
                           _row_tile(nb * T, OUT_TM))
    return (hp.reshape(B, S, D), hs.reshape(nb, T, D),
            jnp.stack(st_p), jnp.stack(st_s),
            jnp.stack(kp_rows), jnp.stack(vp_rows), jnp.stack(ks_rows), jnp.stack(vs_rows))
```

```python
import functools

import numpy as np
import jax
import jax.numpy as jnp
from jax import lax
from jax.experimental import pallas as pl
from jax.experimental.pallas import tpu as pltpu

F32 = jnp.float32
BF16 = jnp.bfloat16

N_MIXERS = 2
HEADS = 16
CHUNK = 64
ATTN_WINDOW = 8 * CHUNK
MAX_REL = 256
LN_EPS = 1e-5
RMS_EPS = 1e-6
LB_TINY = 1e-30
NEG_BIG = -1e30

LANES = 128
SUBLANES = 8
BF16_ROWS = 16
HGRN_CHUNK = 128
HGRN_HEADS_PROMPT = 16
HGRN_HEADS_SAMPLE = 16
HGRN_GROUP = 1
ATTN_TQ = ATTN_WINDOW
ATTN_SUB = 128
ATTN_KW = ATTN_SUB + ATTN_WINDOW
ATTN_HALF = 256
ATTN_HEADS = 16
OUT_TM = 512
OUT_SUB = 128
VMEM_LIMIT = 48 * 1024 * 1024
VMEM_LIMIT_BIG = 58 * 1024 * 1024
LOG2E = 1.4426950408889634

_DIMS_NT = (((1,), (1,)), ((), ()))


def _silu(x):
    h = 0.5 * x
    return h * jnp.tanh(h) + h


def _dot_nt(x, y):
    return lax.dot_general(x, y, _DIMS_NT, preferred_element_type=F32)


def _forget_lower_bound(lb_ref, layer_j):
    lbp = lb_ref[...].astype(F32)
    pe = jnp.exp(lbp - jnp.max(lbp, axis=0, keepdims=True))
    prob = pe / jnp.sum(pe, axis=0, keepdims=True)
    cs = prob[0:1]
    for i in range(1, layer_j + 1):
        cs = cs + prob[i:i + 1]
    return cs - prob[0:1]


def _mm_kernel(x_ref, w_ref, o_ref, w16_ref, *, silu):
    @pl.when(pl.program_id(1) == 0)
    def _():
        w16_ref[...] = w_ref[...].astype(BF16)

    acc = jnp.dot(x_ref[...].astype(BF16), w16_ref[...], preferred_element_type=F32)
    o_ref[...] = (_silu(acc) if silu else acc).astype(o_ref.dtype)


def _in_proj(x16, w, layer, col0, n_cols, tm, tn, mode="plain", out_dtype=F32):
    M, K = x16.shape
    assert M % tm == 0 and n_cols % tn == 0 and col0 % tn == 0
    cb0 = col0 // tn
    return pl.pallas_call(
        functools.partial(_mm_kernel, silu=mode == "silu"),
        out_shape=jax.ShapeDtypeStruct((M, n_cols), out_dtype),
        grid=(n_cols // tn, M // tm),
        in_specs=[pl.BlockSpec((tm, K), lambda j, i: (i, 0)),
                  pl.BlockSpec((None, K, tn), lambda j, i: (layer, 0, cb0 + j))],
        out_specs=pl.BlockSpec((tm, tn), lambda j, i: (i, j)),
        scratch_shapes=[pltpu.VMEM((K, tn), BF16)],
        compiler_params=pltpu.CompilerParams(
            dimension_semantics=("arbitrary", "arbitrary"),
            vmem_limit_bytes=VMEM_LIMIT_BIG if tm > 1024 else VMEM_LIMIT),
        name="in_proj_" + mode,
    )(x16, w)


def _layer_norm_rows(y, g, b):
    mu = jnp.mean(y, axis=-1, keepdims=True)
    yc = y - mu
    var = jnp.mean(yc * yc, axis=-1, keepdims=True)
    return yc * lax.rsqrt(var + LN_EPS) * g + b


def _out_ln_kernel(m_ref, w_ref, h_ref, g_ref, b_ref, o_ref, o16_ref, *, alpha, sub):
    for r0 in range(0, m_ref.shape[0], sub):
        rows = slice(r0, r0 + sub)
        y = alpha * h_ref[rows] + jnp.dot(m_ref[rows], w_ref[...], preferred_element_type=F32)
        out = _layer_norm_rows(y, g_ref[...], b_ref[...])
        o_ref[rows] = out
        o16_ref[rows] = out.astype(BF16)


def _out_ln(mix16, w16, layer, h, g, b, alpha, tm):
    M, D = h.shape
    assert M % tm == 0
    row = lambda i: (i, 0)
    const = lambda i: (0, 0)
    return pl.pallas_call(
        functools.partial(_out_ln_kernel, alpha=alpha, sub=min(tm, OUT_SUB)),
        out_shape=(jax.ShapeDtypeStruct((M, D), F32), jax.ShapeDtypeStruct((M, D), BF16)),
        grid=(M // tm,),
        in_specs=[pl.BlockSpec((tm, D), row),
                  pl.BlockSpec((None, D, D), lambda i: (layer, 0, 0), pipeline_mode=pl.Buffered(1)),
                  pl.BlockSpec((tm, D), row), pl.BlockSpec((1, D), const), pl.BlockSpec((1, D), const)],
        out_specs=(pl.BlockSpec((tm, D), row), pl.BlockSpec((tm, D), row)),
        compiler_params=pltpu.CompilerParams(
            dimension_semantics=("arbitrary",), vmem_limit_bytes=VMEM_LIMIT),
        name="out_proj_ln",
    )(mix16, w16, h, g.reshape(1, D), b.reshape(1, D))


def _head(x, h):
    return x[:, h * LANES:(h + 1) * LANES]


def _hgrn_stage1(qs, z, lb, sub, C):
    W = qs.shape[1]
    heads = W // LANES
    pair = lambda x, y: [_dot_nt(_head(x, h), _head(y, h)) for h in range(heads)]
    lb_c = jnp.maximum(lb, LB_TINY)
    e = jnp.exp(-jnp.abs(z))
    pos = z >= 0
    inv = 1.0 / (1.0 + e)
    a = jnp.log2(jnp.where(pos, 1.0 + lb_c * e, e + lb_c) * inv)
    kk = (1.0 - lb) * (jnp.where(pos, e, 1.0) * inv)
    qs16 = qs.astype(BF16)
    kk16 = kk.astype(BF16)

    V = C // SUBLANES
    shape3 = (V, SUBLANES, W)
    p3 = a.reshape(shape3)
    qs3 = qs.reshape(shape3)
    kk3 = kk.reshape(shape3)
    parts = [pair(qs16, kk16)]
    for n in (1, 2, 4):
        if n == 1:
            bn = jnp.where((sub & 1) != 0, pltpu.roll(p3, 1, 1), p3)
        elif n == 2:
            bn = jnp.where(sub < 4, jnp.broadcast_to(p3[:, 1:2], shape3), jnp.broadcast_to(p3[:, 5:6], shape3))
        else:
            bn = jnp.broadcast_to(p3[:, 3:4], shape3)
        right = (sub & n) != 0
        ex = jnp.exp2(jnp.where(right, p3, bn - p3))
        un = (jnp.where(right, qs3, kk3) * ex).reshape(C, W).astype(BF16)
        parts.append(pair(un, un))
        p3 = p3 + jnp.where(right, bn, 0.0)
    p = p3.reshape(C, W)

    n = SUBLANES
    while n < C:
        q_rows, k_rows, p_rows = [], [], []
        for lo in range(0, C, 2 * n):
            mid, hi = lo + n, lo + 2 * n
            bn = p[mid - 1:mid]
            k_left = kk[lo:mid] * jnp.exp2(bn - p[lo:mid])
            q_right = qs[mid:hi] * jnp.exp2(p[mid:hi])
            if n >= BF16_ROWS:
                q_rows += [qs16[lo:mid], q_right.astype(BF16)]
                k_rows += [k_left.astype(BF16), kk16[mid:hi]]
            else:
                q_rows += [qs[lo:mid], q_right]
                k_rows += [k_left, kk[mid:hi]]
            p_rows += [p[lo:mid], p[mid:hi] + bn]
        parts.append(pair(jnp.concatenate(q_rows, axis=0).astype(BF16),
                          jnp.concatenate(k_rows, axis=0).astype(BF16)))
        p = jnp.concatenate(p_rows, axis=0)
        n *= 2

    b_end = p[C - 1:C, :]
    qe = (qs * jnp.exp2(p)).astype(BF16)
    kd = (kk * jnp.exp2(b_end - p)).astype(BF16)
    return parts, qe, kd, jnp.exp2(b_end)


def _hgrn_stage2(parts, qe, kd, dec, v16, scale, st_ref, store_mix, masks, C, h0):
    for i in range(qe.shape[1] // LANES):
        h = h0 + i
        amat = jnp.where(masks[0], parts[0][i], 0.0)
        for k in (1, 2, 3):
            amat = jnp.where(masks[k], parts[k][i], amat)
        n, k = SUBLANES, 4
        while n < C:
            rows = []
            for lo in range(0, C, 2 * n):
                mid, hi = lo + n, lo + 2 * n
                rows.append(amat[lo:mid])
                rows.append(jnp.where(masks[k][mid:hi], parts[k][i][mid:hi], amat[mid:hi]))
            amat = jnp.concatenate(rows, axis=0)
            n, k = 2 * n, k + 1

        st = st_ref[h]
        vh = _head(v16, i)
        o = jnp.dot(amat.astype(BF16), vh, preferred_element_type=F32) + _dot_nt(_head(qe, i), st.astype(BF16))
        st_ref[h] = st * _head(dec, i) + jnp.dot(vh.T, _head(kd, i), preferred_element_type=F32)
        ms = jnp.mean(o * o, axis=-1, keepdims=True)
        store_mix(h, o * lax.rsqrt(ms + RMS_EPS) * _head(scale, i))


def _hgrn_kernel(*refs, layer_j, chunk, heads, has_s0):
    if has_s0:
        lvl_ref, lb_ref, ng_ref, q_ref, z_ref, v_ref, g_ref, s0_ref, mix_ref, sfin_ref, st_ref = refs
    else:
        lvl_ref, lb_ref, ng_ref, q_ref, z_ref, v_ref, g_ref, mix_ref, sfin_ref, st_ref = refs
    C = chunk
    c = pl.program_id(2)

    @pl.when(c == 0)
    def _():
        for h in range(heads):
            if has_s0:
                st_ref[h] = s0_ref[0, h].T
            else:
                st_ref[h] = jnp.zeros((LANES, LANES), F32)

    ng = ng_ref[...].astype(F32)
    lvl = lvl_ref[...]
    n_levels = C.bit_length()
    masks = [lvl == k for k in range(n_levels)]
    group = min(HGRN_GROUP, heads)
    gw = group * LANES
    sub = lax.broadcasted_iota(jnp.int32, (1, SUBLANES, gw), 1)
    lb = _forget_lower_bound(lb_ref, layer_j)

    def store_mix(h, val):
        mix_ref[0, :, h * LANES:(h + 1) * LANES] = val.astype(mix_ref.dtype)

    pending = None
    for h0 in range(0, heads, group):
        cols = slice(h0 * LANES, h0 * LANES + gw)
        stage1 = _hgrn_stage1(q_ref[0, :, cols], z_ref[0, :, cols], lb[:, cols], sub, C)
        if pending is not None:
            _hgrn_stage2(*pending)
        pending = (*stage1, v_ref[0, :, cols], ng[:, cols] * g_ref[0, :, cols], st_ref, store_mix, masks, C, h0)
    _hgrn_stage2(*pending)

    @pl.when(c == pl.num_programs(2) - 1)
    def _():
        for h in range(heads):
            sfin_ref[0, h] = st_ref[h].T


def _level_ids(C):
    t = np.arange(C)[:, None]
    s = np.arange(C)[None, :]
    x = t ^ s
    lv = np.full((C, C), -1, np.int32)
    lv[x == 0] = 0
    n, k = 1, 1
    while n < C:
        lv[(x >= n) & (x < 2 * n) & (t > s)] = k
        n, k = 2 * n, k + 1
    return jnp.asarray(lv)


def _hgrn_mixer(qs, z, v16, gs, hgrn_lb, norm_g, layer_j, chunk, heads, s0=None):
    nb, T, D = qs.shape
    H = HEADS
    assert D // H == LANES and T % chunk == 0 and H % heads == 0 and chunk % BF16_ROWS == 0
    has_s0 = s0 is not None
    hg = H // heads
    w = heads * LANES
    tok_blk = pl.BlockSpec((1, chunk, w), lambda b, h, c: (b, c, h))
    st_blk = pl.BlockSpec((1, heads, LANES, LANES), lambda b, h, c: (b, h, 0, 0))
    in_specs = [
        pl.BlockSpec((chunk, chunk), lambda b, h, c: (0, 0)),
        pl.BlockSpec((hgrn_lb.shape[0], w), lambda b, h, c: (0, h)),
        pl.BlockSpec((1, w), lambda b, h, c: (0, h)),
        tok_blk, tok_blk, tok_blk, tok_blk,
    ]
    args = [_level_ids(chunk), hgrn_lb, norm_g.reshape(1, D), qs, z, v16, gs]
    if has_s0:
        in_specs.append(st_blk)
        args.append(s0)
    return pl.pallas_call(
        functools.partial(_hgrn_kernel, layer_j=layer_j, chunk=chunk, heads=heads, has_s0=has_s0),
        out_shape=(jax.ShapeDtypeStruct((nb, T, D), BF16),
                   jax.ShapeDtypeStruct((nb, H, LANES, LANES), F32)),
        grid=(nb, hg, T // chunk),
        in_specs=in_specs,
        out_specs=(pl.BlockSpec((1, chunk, w), lambda b, h, c: (b, c, h)), st_blk),
        scratch_shapes=[pltpu.VMEM((heads, LANES, LANES), F32)],
        compiler_params=pltpu.CompilerParams(
            dimension_semantics=("arbitrary", "arbitrary", "arbitrary"), vmem_limit_bytes=VMEM_LIMIT),
        name="hgrn_sample" if has_s0 else "hgrn_prompt",
    )(*args)


def _toeplitz_bias(table, rows, cols):
    H, n_rel = table.shape
    assert n_rel == MAX_REL + CHUNK
    d_hi = ATTN_WINDOW + rows - 1
    d_lo = ATTN_WINDOW - (cols - 1)
    left = d_hi - MAX_REL
    right = -(CHUNK - 1) - d_lo
    assert left >= 0 and right >= 0
    g = jnp.pad(table[:, ::-1], ((0, 0), (left, right)), mode="edge")
    n = rows + cols - 1
    g = jnp.pad(g, ((0, 0), (0, 1)))
    flat = jnp.tile(g, (1, rows))[:, rows - 1: rows - 1 + rows * n]
    return flat.reshape(H, rows, n)[:, :, :cols]


def _softmax2_pv(scores, values):
    m = functools.reduce(jnp.maximum, [jnp.max(s, axis=-1, keepdims=True) for s in scores])
    ps = [jnp.exp2(s - m) for s in scores]
    den = functools.reduce(lambda x, y: x + y, [jnp.sum(p, axis=-1, keepdims=True) for p in ps])
    o = functools.reduce(lambda x, y: x + y,
                         [jnp.dot(p.astype(BF16), v, preferred_element_type=F32) for p, v in zip(ps, values)])
    return o * (1.0 / den)


def _attn_prompt_kernel(bias_ref, q_ref, k_ref, v_ref, g_ref, mix_ref, kprev_ref, vprev_ref, *, qscale, heads):
    i = pl.program_id(2)
    tq = q_ref.shape[1]

    @pl.when(i == 0)
    def _():
        kprev_ref[...] = jnp.zeros_like(kprev_ref)
        vprev_ref[...] = jnp.zeros_like(vprev_ref)

    has_prev = i > 0

    halves = range(0, tq, ATTN_HALF)
    kwin = ATTN_HALF + ATTN_WINDOW

    def scores(h):
        sl = slice(h * LANES, (h + 1) * LANES)
        q16 = (q_ref[0, :, sl].astype(F32) * qscale).astype(BF16)
        kcat = jnp.concatenate([kprev_ref[:, sl], k_ref[0, :, sl].astype(BF16)], axis=0)
        return [_dot_nt(kcat[c0:c0 + kwin], q16[c0:c0 + ATTN_HALF]) for c0 in halves]

    s_next = scores(0)
    for h in range(heads):
        sl = slice(h * LANES, (h + 1) * LANES)
        s_halves = s_next
        if h + 1 < heads:
            s_next = scores(h + 1)
        bias = bias_ref[h]
        vcat_t = jnp.concatenate([vprev_ref[:, sl], v_ref[0, :, sl].astype(BF16)], axis=0).T
        o_cols = []
        for c0, s_half in zip(halves, s_halves):
            p_cols, inv_cols = [], []
            for r in range(0, ATTN_HALF, ATTN_SUB):
                n_prev = ATTN_WINDOW - c0 - r
                s = s_half[r:r + ATTN_KW, r:r + ATTN_SUB] + bias
                s = jnp.concatenate([jnp.where(has_prev, s[:n_prev], NEG_BIG), s[n_prev:]], axis=0)
                p = jnp.exp2(s - jnp.max(s, axis=0, keepdims=True))
                inv_cols.append(1.0 / jnp.sum(p, axis=0, keepdims=True))
                pieces = [p.astype(BF16)]
                if r:
                    pieces.insert(0, jnp.zeros((r, ATTN_SUB), BF16))
                if kwin - ATTN_KW - r:
                    pieces.append(jnp.zeros((kwin - ATTN_KW - r, ATTN_SUB), BF16))
                p_cols.append(jnp.concatenate(pieces, axis=0))
            o_half = jnp.dot(vcat_t[:, c0:c0 + kwin], jnp.concatenate(p_cols, axis=1), preferred_element_type=F32)
            o_cols.append(o_half * jnp.concatenate(inv_cols, axis=1))
        o = jnp.concatenate(o_cols, axis=1).T
        mix_ref[0, :, sl] = (o * _silu(g_ref[0, :, sl].astype(F32))).astype(mix_ref.dtype)

    kprev_ref[...] = k_ref[0].astype(BF16)
    vprev_ref[...] = v_ref[0].astype(BF16)


def _attn_prompt(proj, bias2):
    B, S, D4 = proj.shape
    D = D4 // 4
    H = HEADS
    tq = ATTN_TQ
    heads = ATTN_HEADS
    hg = H // heads
    w = heads * LANES
    assert D // H == LANES and S % tq == 0 and H % heads == 0
    col = lambda off: (lambda b, h, i: (b, i, off * hg + h))
    blk = (1, tq, w)
    return pl.pallas_call(
        functools.partial(_attn_prompt_kernel, qscale=float(LANES) ** -0.5 * LOG2E, heads=heads),
        out_shape=jax.ShapeDtypeStruct((B, S, D), BF16),
        grid=(B, hg, S // tq),
        in_specs=[pl.BlockSpec((heads, ATTN_KW, ATTN_SUB), lambda b, h, i: (h, 0, 0)),
                  pl.BlockSpec(blk, col(0)), pl.BlockSpec(blk, col(1)),
                  pl.BlockSpec(blk, col(2)), pl.BlockSpec(blk, col(3))],
        out_specs=pl.BlockSpec(blk, lambda b, h, i: (b, i, h)),
        scratch_shapes=[pltpu.VMEM((tq, w), BF16), pltpu.VMEM((tq, w), BF16)],
        compiler_params=pltpu.CompilerParams(
            dimension_semantics=("arbitrary", "arbitrary", "arbitrary"), vmem_limit_bytes=VMEM_LIMIT),
        name="attn_prompt",
    )(bias2, proj, proj, proj, proj)


def _attn_sample_kernel(bc_ref, bn_ref, q_ref, kn_ref, vn_ref, g_ref, kc_ref, vc_ref, mix_ref, *, qscale, heads):
    win = kc_ref.shape[1] // heads

    def scores(h):
        sl = slice(h * LANES, (h + 1) * LANES)
        kc = kc_ref[0, pl.ds(h, win, stride=heads), :].astype(BF16)
        q16 = (q_ref[0, :, sl].astype(F32) * qscale).astype(BF16)
        return [_dot_nt(q16, kc) + bc_ref[h], _dot_nt(q16, kn_ref[0, :, sl].astype(BF16)) + bn_ref[h]]

    s_next = scores(0)
    for h in range(heads):
        sl = slice(h * LANES, (h + 1) * LANES)
        s_parts = s_next
        if h + 1 < heads:
            s_next = scores(h + 1)
        vc = vc_ref[0, pl.ds(h, win, stride=heads), :].astype(BF16)
        o = _softmax2_pv(s_parts, [vc, vn_ref[0, :, sl].astype(BF16)])
        mix_ref[0, :, sl] = (o * _silu(g_ref[0, :, sl].astype(F32))).astype(mix_ref.dtype)


def _attn_sample(proj, cache_k, cache_v, layer_j, bias_c2, bias_n2):
    nb, T, D4 = proj.shape
    D = D4 // 4
    H = HEADS
    W = cache_k.shape[2] // H
    new = lambda off: (lambda b: (b, 0, off))
    blk = (1, T, D)
    cblk = (None, 1, W * H, LANES)
    return pl.pallas_call(
        functools.partial(_attn_sample_kernel, qscale=float(LANES) ** -0.5 * LOG2E, heads=H),
        out_shape=jax.ShapeDtypeStruct((nb, T, D), BF16),
        grid=(nb,),
        in_specs=[pl.BlockSpec((H, T, W), lambda b: (0, 0, 0)),
                  pl.BlockSpec((H, T, T), lambda b: (0, 0, 0)),
                  pl.BlockSpec(blk, new(0)), pl.BlockSpec(blk, new(1)),
                  pl.BlockSpec(blk, new(2)), pl.BlockSpec(blk, new(3)),
                  pl.BlockSpec(cblk, lambda b: (layer_j, b, 0, 0)),
                  pl.BlockSpec(cblk, lambda b: (layer_j, b, 0, 0))],
        out_specs=pl.BlockSpec(blk, lambda b: (b, 0, 0)),
        compiler_params=pltpu.CompilerParams(
            dimension_semantics=("arbitrary",), vmem_limit_bytes=VMEM_LIMIT),
        name="attn_sample",
    )(bias_c2, bias_n2, proj, proj, proj, proj, cache_k, cache_v)


def _row_tile(m, cap):
    t = cap
    while m % t:
        t //= 2
    return t


def kernel(x_prompt, x_sample, state_hgrn, cache_attn_k, cache_attn_v, w_in, w_out, ln_g, ln_b,
           hgrn_lb, hgrn_norm_g, attn_rel_bias):
    B, S, D = x_prompt.shape
    nb, T, _ = x_sample.shape
    depth = w_in.shape[0]
    H = HEADS
    hd = D // H
    alpha = (2 * depth) ** 0.25
    kv_win = cache_attn_k.shape[2]
    prompt_win = min(ATTN_WINDOW, S)
    assert kv_win == ATTN_WINDOW

    w_out16 = w_out.astype(BF16)
    hp = x_prompt.reshape(B * S, D)
    hs = x_sample.reshape(nb * T, D)
    hp16, hs16 = hp, hs
    tm_p = _row_tile(B * S, 1024)
    tm_s = _row_tile(nb * T, 512)
    tn = 1024
    cache_k = cache_attn_k.reshape(cache_attn_k.shape[0], nb, kv_win * H, hd)
    cache_v = cache_attn_v.reshape(cache_attn_v.shape[0], nb, kv_win * H, hd)

    qc = (np.arange(ATTN_SUB)[:, None] + ATTN_WINDOW) // CHUNK
    kc = np.arange(ATTN_KW)[None, :] // CHUNK
    band_ok = jnp.asarray((kc <= qc) & (kc >= qc - ATTN_WINDOW // CHUNK))

    st_p, st_s, kp_rows, vp_rows, ks_rows, vs_rows = [], [], [], [], [], []
    for layer in range(depth):
        j = layer // N_MIXERS
        if layer % N_MIXERS == 0:
            def sections(x16, nseq, tm):
                sec = lambda y: y.reshape(nseq, -1, D)
                return (sec(_in_proj(x16, w_in, layer, 0, D, tm, tn, "silu")),
                        sec(_in_proj(x16, w_in, layer, D, D, tm, tn, "plain")),
                        sec(_in_proj(x16, w_in, layer, 2 * D, D, tm, tn, "plain", BF16)),
                        sec(_in_proj(x16, w_in, layer, 3 * D, D, tm, tn, "silu")))

            mix_p, s_fin = _hgrn_mixer(*sections(hp16, B, tm_p), hgrn_lb, hgrn_norm_g[j], j, min(HGRN_CHUNK, S),
                                       HGRN_HEADS_PROMPT)
            mix_s, s_new = _hgrn_mixer(*sections(hs16, nb, tm_s), hgrn_lb, hgrn_norm_g[j], j, T,
                                       HGRN_HEADS_SAMPLE, s0=state_hgrn[j])
            st_p.append(s_fin.astype(state_hgrn.dtype))
            st_s.append(s_new.astype(state_hgrn.dtype))
        else:
            proj_p = _in_proj(hp16, w_in, layer, 0, 4 * D, _row_tile(B * S, 2048), tn, "plain",
                              BF16).reshape(B, S, 4 * D)
            proj_s = _in_proj(hs16, w_in, layer, 0, 4 * D, tm_s, tn, "plain", BF16).reshape(nb, T, 4 * D)
            toep2 = _toeplitz_bias(attn_rel_bias[j].astype(F32), ATTN_SUB, ATTN_KW) * LOG2E
            mix_p = _attn_prompt(proj_p, jnp.swapaxes(jnp.where(band_ok, toep2, NEG_BIG), 1, 2))
            mix_s = _attn_sample(proj_s, cache_k, cache_v, j,
                                 toep2[:, :T, :kv_win], toep2[:, :T, kv_win:kv_win + T])
            out_dt = x_prompt.dtype
            kp_rows.append(proj_p[:, S - prompt_win:, D:2 * D].astype(out_dt).reshape(B, prompt_win, H, hd))
            vp_rows.append(proj_p[:, S - prompt_win:, 2 * D:3 * D].astype(out_dt).reshape(B, prompt_win, H, hd))
            ks_rows.append(proj_s[:, :, D:2 * D].astype(out_dt).reshape(nb, T, H, hd))
            vs_rows.append(proj_s[:, :, 2 * D:3 * D].astype(out_dt).reshape(nb, T, H, hd))
        hp, hp16 = _out_ln(mix_p.reshape(B * S, D), w_out16, layer, hp, ln_g[layer], ln_b[layer], alpha,
                           _row_tile(B * S, OUT_TM))
        hs, hs16 = _out_ln(mix_s.reshape(nb * T, D), w_out16, layer, hs, ln_g[layer], ln_b[layer], alpha,
                           _row_tile(nb * T, OUT_TM))
    return (hp.reshape(B, S, D), hs.reshape(nb, T, D),
            jnp.stack(st_p), jnp.stack(st_s),
            jnp.stack(kp_rows), jnp.stack(vp_rows), jnp.stack(ks_rows), jnp.stack(vs_rows))
```

```python
import functools

import numpy as np
import jax
import jax.numpy as jnp
from jax import lax
from jax.experimental import pallas as pl
from jax.experimental.pallas import tpu as pltpu

F32 = jnp.float32
BF16 = jnp.bfloat16

N_MIXERS = 2
HEADS = 16
CHUNK = 64
ATTN_WINDOW = 8 * CHUNK
MAX_REL = 256
LN_EPS = 1e-5
RMS_EPS = 1e-6
LB_TINY = 1e-30
NEG_BIG = -1e30

LANES = 128
SUBLANES = 8
BF16_ROWS = 16
HGRN_CHUNK = 128
HGRN_HEADS_PROMPT = 16
HGRN_HEADS_SAMPLE = 16
HGRN_GROUP = 1
ATTN_TQ = ATTN_WINDOW
ATTN_SUB = 128
ATTN_KW = ATTN_SUB + ATTN_WINDOW
ATTN_HALF = 256
ATTN_HEADS = 16
OUT_TM = 512
OUT_SUB = 128
VMEM_LIMIT = 48 * 1024 * 1024
VMEM_LIMIT_BIG = 58 * 1024 * 1024
LOG2E = 1.4426950408889634

_DIMS_NT = (((1,), (1,)), ((), ()))


def _silu(x):
    h = 0.5 * x
    return h * jnp.tanh(h) + h


def _dot_nt(x, y):
    return lax.dot_general(x, y, _DIMS_NT, preferred_element_type=F32)


def _forget_lower_bound(lb_ref, layer_j):
    lbp = lb_ref[...].astype(F32)
    pe = jnp.exp(lbp - jnp.max(lbp, axis=0, keepdims=True))
    prob = pe / jnp.sum(pe, axis=0, keepdims=True)
    cs = prob[0:1]
    for i in range(1, layer_j + 1):
        cs = cs + prob[i:i + 1]
    return cs - prob[0:1]


def _mm_kernel(x_ref, w_ref, o_ref, w16_ref, *, silu):
    @pl.when(pl.program_id(1) == 0)
    def _():
        w16_ref[...] = w_ref[...].astype(BF16)

    acc = jnp.dot(x_ref[...].astype(BF16), w16_ref[...], preferred_element_type=F32)
    o_ref[...] = (_silu(acc) if silu else acc).astype(o_ref.dtype)


def _in_proj(x16, w, layer, col0, n_cols, tm, tn, mode="plain", out_dtype=F32):
    M, K = x16.shape
    assert M % tm == 0 and n_cols % tn == 0 and col0 % tn == 0
    cb0 = col0 // tn
    return pl.pallas_call(
        functools.partial(_mm_kernel, silu=mode == "silu"),
        out_shape=jax.ShapeDtypeStruct((M, n_cols), out_dtype),
        grid=(n_cols // tn, M // tm),
        in_specs=[pl.BlockSpec((tm, K), lambda j, i: (i, 0)),
                  pl.BlockSpec((None, K, tn), lambda j, i: (layer, 0, cb0 + j))],
        out_specs=pl.BlockSpec((tm, tn), lambda j, i: (i, j)),
        scratch_shapes=[pltpu.VMEM((K, tn), BF16)],
        compiler_params=pltpu.CompilerParams(
            dimension_semantics=("arbitrary", "arbitrary"),
            vmem_limit_bytes=VMEM_LIMIT_BIG if tm > 1024 else VMEM_LIMIT),
        name="in_proj_" + mode,
    )(x16, w)


def _layer_norm_rows(y, g, b):
    mu = jnp.mean(y, axis=-1, keepdims=True)
    yc = y - mu
    var = jnp.mean(yc * yc, axis=-1, keepdims=True)
    return yc * lax.rsqrt(var + LN_EPS) * g + b


def _out_ln_kernel(m_ref, w_ref, h_ref, g_ref, b_ref, o_ref, o16_ref, *, alpha, sub):
    for r0 in range(0, m_ref.shape[0], sub):
        rows = slice(r0, r0 + sub)
        y = alpha * h_ref[rows] + jnp.dot(m_ref[rows], w_ref[...], preferred_element_type=F32)
        out = _layer_norm_rows(y, g_ref[...], b_ref[...])
        o_ref[rows] = out
        o16_ref[rows] = out.astype(BF16)


def _out_ln(mix16, w16, layer, h, g, b, alpha, tm):
    M, D = h.shape
    assert M % tm == 0
    row = lambda i: (i, 0)
    const = lambda i: (0, 0)
    return pl.pallas_call(
        functools.partial(_out_ln_kernel, alpha=alpha, sub=min(tm, OUT_SUB)),
        out_shape=(jax.ShapeDtypeStruct((M, D), F32), jax.ShapeDtypeStruct((M, D), BF16)),
        grid=(M // tm,),
        in_specs=[pl.BlockSpec((tm, D), row),
                  pl.BlockSpec((None, D, D), lambda i: (layer, 0, 0), pipeline_mode=pl.Buffered(1)),
                  pl.BlockSpec((tm, D), row), pl.BlockSpec((1, D), const), pl.BlockSpec((1, D), const)],
        out_specs=(pl.BlockSpec((tm, D), row), pl.BlockSpec((tm, D), row)),
        compiler_params=pltpu.CompilerParams(
            dimension_semantics=("arbitrary",), vmem_limit_bytes=VMEM_LIMIT),
        name="out_proj_ln",
    )(mix16, w16, h, g.reshape(1, D), b.reshape(1, D))


def _head(x, h):
    return x[:, h * LANES:(h + 1) * LANES]


def _hgrn_stage1(qs, z, lb, sub, C):
    W = qs.shape[1]
    heads = W // LANES
    pair = lambda x, y: [_dot_nt(_head(x, h), _head(y, h)) for h in range(heads)]
    lb_c = jnp.maximum(lb, LB_TINY)
    e = jnp.exp(-jnp.abs(z))
    pos = z >= 0
    inv = 1.0 / (1.0 + e)
    a = jnp.log2(jnp.where(pos, 1.0 + lb_c * e, e + lb_c) * inv)
    kk = (1.0 - lb) * (jnp.where(pos, e, 1.0) * inv)
    qs16 = qs.astype(BF16)
    kk16 = kk.astype(BF16)

    V = C // SUBLANES
    shape3 = (V, SUBLANES, W)
    p3 = a.reshape(shape3)
    qs3 = qs.reshape(shape3)
    kk3 = kk.reshape(shape3)
    parts = [pair(qs16, kk16)]
    for n in (1, 2, 4):
        if n == 1:
            bn = jnp.where((sub & 1) != 0, pltpu.roll(p3, 1, 1), p3)
        elif n == 2:
            bn = jnp.where(sub < 4, jnp.broadcast_to(p3[:, 1:2], shape3), jnp.broadcast_to(p3[:, 5:6], shape3))
        else:
            bn = jnp.broadcast_to(p3[:, 3:4], shape3)
        right = (sub & n) != 0
        ex = jnp.exp2(jnp.where(right, p3, bn - p3))
        qn = (qs3 * ex).reshape(C, W).astype(BF16)
        kn = (kk3 * ex).reshape(C, W).astype(BF16)
        parts.append(pair(qn, kn))
        p3 = p3 + jnp.where(right, bn, 0.0)
    p = p3.reshape(C, W)

    n = SUBLANES
    while n < C:
        q_rows, k_rows, p_rows = [], [], []
        for lo in range(0, C, 2 * n):
            mid, hi = lo + n, lo + 2 * n
            bn = p[mid - 1:mid]
            k_left = kk[lo:mid] * jnp.exp2(bn - p[lo:mid])
            q_right = qs[mid:hi] * jnp.exp2(p[mid:hi])
            if n >= BF16_ROWS:
                q_rows += [qs16[lo:mid], q_right.astype(BF16)]
                k_rows += [k_left.astype(BF16), kk16[mid:hi]]
            else:
                q_rows += [qs[lo:mid], q_right]
                k_rows += [k_left, kk[mid:hi]]
            p_rows += [p[lo:mid], p[mid:hi] + bn]
        parts.append(pair(jnp.concatenate(q_rows, axis=0).astype(BF16),
                          jnp.concatenate(k_rows, axis=0).astype(BF16)))
        p = jnp.concatenate(p_rows, axis=0)
        n *= 2

    b_end = p[C - 1:C, :]
    qe = (qs * jnp.exp2(p)).astype(BF16)
    kd = (kk * jnp.exp2(b_end - p)).astype(BF16)
    return parts, qe, kd, jnp.exp2(b_end)


def _hgrn_stage2(parts, qe, kd, dec, v16, scale, st_ref, store_mix, masks, C, h0):
    for i in range(qe.shape[1] // LANES):
        h = h0 + i
        amat = jnp.where(masks[0], parts[0][i], 0.0)
        for k in (1, 2, 3):
            amat = jnp.where(masks[k], parts[k][i], amat)
        n, k = SUBLANES, 4
        while n < C:
            rows = []
            for lo in range(0, C, 2 * n):
                mid, hi = lo + n, lo + 2 * n
                rows.append(amat[lo:mid])
                rows.append(jnp.where(masks[k][mid:hi], parts[k][i][mid:hi], amat[mid:hi]))
            amat = jnp.concatenate(rows, axis=0)
            n, k = 2 * n, k + 1

        st = st_ref[h]
        vh = _head(v16, i)
        o = jnp.dot(amat.astype(BF16), vh, preferred_element_type=F32) + _dot_nt(_head(qe, i), st.astype(BF16))
        st_ref[h] = st * _head(dec, i) + jnp.dot(vh.T, _head(kd, i), preferred_element_type=F32)
        ms = jnp.mean(o * o, axis=-1, keepdims=True)
        store_mix(h, o * lax.rsqrt(ms + RMS_EPS) * _head(scale, i))


def _hgrn_kernel(*refs, layer_j, chunk, heads, has_s0):
    if has_s0:
        lvl_ref, lb_ref, ng_ref, q_ref, z_ref, v_ref, g_ref, s0_ref, mix_ref, sfin_ref, st_ref = refs
    else:
        lvl_ref, lb_ref, ng_ref, q_ref, z_ref, v_ref, g_ref, mix_ref, sfin_ref, st_ref = refs
    C = chunk
    c = pl.program_id(2)

    @pl.when(c == 0)
    def _():
        for h in range(heads):
            if has_s0:
                st_ref[h] = s0_ref[0, h].T
            else:
                st_ref[h] = jnp.zeros((LANES, LANES), F32)

    ng = ng_ref[...].astype(F32)
    lvl = lvl_ref[...]
    n_levels = C.bit_length()
    masks = [lvl == k for k in range(n_levels)]
    group = min(HGRN_GROUP, heads)
    gw = group * LANES
    sub = lax.broadcasted_iota(jnp.int32, (1, SUBLANES, gw), 1)
    lb = _forget_lower_bound(lb_ref, layer_j)

    def store_mix(h, val):
        mix_ref[0, :, h * LANES:(h + 1) * LANES] = val.astype(mix_ref.dtype)

    pending = None
    for h0 in range(0, heads, group):
        cols = slice(h0 * LANES, h0 * LANES + gw)
        stage1 = _hgrn_stage1(q_ref[0, :, cols], z_ref[0, :, cols], lb[:, cols], sub, C)
        if pending is not None:
            _hgrn_stage2(*pending)
        pending = (*stage1, v_ref[0, :, cols], ng[:, cols] * g_ref[0, :, cols], st_ref, store_mix, masks, C, h0)
    _hgrn_stage2(*pending)

    @pl.when(c == pl.num_programs(2) - 1)
    def _():
        for h in range(heads):
            sfin_ref[0, h] = st_ref[h].T


def _level_ids(C):
    t = np.arange(C)[:, None]
    s = np.arange(C)[None, :]
    x = t ^ s
    lv = np.full((C, C), -1, np.int32)
    lv[x == 0] = 0
    n, k = 1, 1
    while n < C:
        lv[(x >= n) & (x < 2 * n) & (t > s)] = k
        n, k = 2 * n, k + 1
    return jnp.asarray(lv)


def _hgrn_mixer(qs, z, v16, gs, hgrn_lb, norm_g, layer_j, chunk, heads, s0=None):
    nb, T, D = qs.shape
    H = HEADS
    assert D // H == LANES and T % chunk == 0 and H % heads == 0 and chunk % BF16_ROWS == 0
    has_s0 = s0 is not None
    hg = H // heads
    w = heads * LANES
    tok_blk = pl.BlockSpec((1, chunk, w), lambda b, h, c: (b, c, h))
    st_blk = pl.BlockSpec((1, heads, LANES, LANES), lambda b, h, c: (b, h, 0, 0))
    in_specs = [
        pl.BlockSpec((chunk, chunk), lambda b, h, c: (0, 0)),
        pl.BlockSpec((hgrn_lb.shape[0], w), lambda b, h, c: (0, h)),
        pl.BlockSpec((1, w), lambda b, h, c: (0, h)),
        tok_blk, tok_blk, tok_blk, tok_blk,
    ]
    args = [_level_ids(chunk), hgrn_lb, norm_g.reshape(1, D), qs, z, v16, gs]
    if has_s0:
        in_specs.append(st_blk)
        args.append(s0)
    return pl.pallas_call(
        functools.partial(_hgrn_kernel, layer_j=layer_j, chunk=chunk, heads=heads, has_s0=has_s0),
        out_shape=(jax.ShapeDtypeStruct((nb, T, D), BF16),
                   jax.ShapeDtypeStruct((nb, H, LANES, LANES), F32)),
        grid=(nb, hg, T // chunk),
        in_specs=in_specs,
        out_specs=(pl.BlockSpec((1, chunk, w), lambda b, h, c: (b, c, h)), st_blk),
        scratch_shapes=[pltpu.VMEM((heads, LANES, LANES), F32)],
        compiler_params=pltpu.CompilerParams(
            dimension_semantics=("arbitrary", "arbitrary", "arbitrary"), vmem_limit_bytes=VMEM_LIMIT),
        name="hgrn_sample" if has_s0 else "hgrn_prompt",
    )(*args)


def _toeplitz_bias(table, rows, cols):
    H, n_rel = table.shape
    assert n_rel == MAX_REL + CHUNK
    d_hi = ATTN_WINDOW + rows - 1
    d_lo = ATTN_WINDOW - (cols - 1)
    left = d_hi - MAX_REL
    right = -(CHUNK - 1) - d_lo
    assert left >= 0 and right >= 0
    g = jnp.pad(table[:, ::-1], ((0, 0), (left, right)), mode="edge")
    n = rows + cols - 1
    g = jnp.pad(g, ((0, 0), (0, 1)))
    flat = jnp.tile(g, (1, rows))[:, rows - 1: rows - 1 + rows * n]
    return flat.reshape(H, rows, n)[:, :, :cols]


def _softmax2_pv(scores, values):
    m = functools.reduce(jnp.maximum, [jnp.max(s, axis=-1, keepdims=True) for s in scores])
    ps = [jnp.exp2(s - m) for s in scores]
    den = functools.reduce(lambda x, y: x + y, [jnp.sum(p, axis=-1, keepdims=True) for p in ps])
    o = functools.reduce(lambda x, y: x + y,
                         [jnp.dot(p.astype(BF16), v, preferred_element_type=F32) for p, v in zip(ps, values)])
    return o * (1.0 / den)


def _attn_prompt_kernel(bias_ref, q_ref, k_ref, v_ref, g_ref, mix_ref, kprev_ref, vprev_ref, *, qscale, heads):
    i = pl.program_id(2)
    tq = q_ref.shape[1]

    @pl.when(i == 0)
    def _():
        kprev_ref[...] = jnp.zeros_like(kprev_ref)
        vprev_ref[...] = jnp.zeros_like(vprev_ref)

    has_prev = i > 0

    halves = range(0, tq, ATTN_HALF)
    kwin = ATTN_HALF + ATTN_WINDOW

    def scores(h):
        sl = slice(h * LANES, (h + 1) * LANES)
        q16 = (q_ref[0, :, sl].astype(F32) * qscale).astype(BF16)
        kcat = jnp.concatenate([kprev_ref[:, sl], k_ref[0, :, sl].astype(BF16)], axis=0)
        return [_dot_nt(kcat[c0:c0 + kwin], q16[c0:c0 + ATTN_HALF]) for c0 in halves]

    s_next = scores(0)
    for h in range(heads):
        sl = slice(h * LANES, (h + 1) * LANES)
        s_halves = s_next
        if h + 1 < heads:
            s_next = scores(h + 1)
        bias = bias_ref[h]
        vcat_t = jnp.concatenate([vprev_ref[:, sl], v_ref[0, :, sl].astype(BF16)], axis=0).T
        o_cols = []
        for c0, s_half in zip(halves, s_halves):
            p_cols, inv_cols = [], []
            for r in range(0, ATTN_HALF, ATTN_SUB):
                n_prev = ATTN_WINDOW - c0 - r
                s = s_half[r:r + ATTN_KW, r:r + ATTN_SUB] + bias
                s = jnp.concatenate([jnp.where(has_prev, s[:n_prev], NEG_BIG), s[n_prev:]], axis=0)
                p = jnp.exp2(s - jnp.max(s, axis=0, keepdims=True))
                inv_cols.append(1.0 / jnp.sum(p, axis=0, keepdims=True))
                pieces = [p.astype(BF16)]
                if r:
                    pieces.insert(0, jnp.zeros((r, ATTN_SUB), BF16))
                if kwin - ATTN_KW - r:
                    pieces.append(jnp.zeros((kwin - ATTN_KW - r, ATTN_SUB), BF16))
                p_cols.append(jnp.concatenate(pieces, axis=0))
            o_half = jnp.dot(vcat_t[:, c0:c0 + kwin], jnp.concatenate(p_cols, axis=1), preferred_element_type=F32)
            o_cols.append(o_half * jnp.concatenate(inv_cols, axis=1))
        o = jnp.concatenate(o_cols, axis=1).T
        mix_ref[0, :, sl] = (o * _silu(g_ref[0, :, sl].astype(F32))).astype(mix_ref.dtype)

    kprev_ref[...] = k_ref[0].astype(BF16)
    vprev_ref[...] = v_ref[0].astype(BF16)


def _attn_prompt(proj, bias2):
    B, S, D4 = proj.shape
    D = D4 // 4
    H = HEADS
    tq = ATTN_TQ
    heads = ATTN_HEADS
    hg = H // heads
    w = heads * LANES
    assert D // H == LANES and S % tq == 0 and H % heads == 0
    col = lambda off: (lambda b, h, i: (b, i, off * hg + h))
    blk = (1, tq, w)
    return pl.pallas_call(
        functools.partial(_attn_prompt_kernel, qscale=float(LANES) ** -0.5 * LOG2E, heads=heads),
        out_shape=jax.ShapeDtypeStruct((B, S, D), BF16),
        grid=(B, hg, S // tq),
        in_specs=[pl.BlockSpec((heads, ATTN_KW, ATTN_SUB), lambda b, h, i: (h, 0, 0)),
                  pl.BlockSpec(blk, col(0)), pl.BlockSpec(blk, col(1)),
                  pl.BlockSpec(blk, col(2)), pl.BlockSpec(blk, col(3))],
        out_specs=pl.BlockSpec(blk, lambda b, h, i: (b, i, h)),
        scratch_shapes=[pltpu.VMEM((tq, w), BF16), pltpu.VMEM((tq, w), BF16)],
        compiler_params=pltpu.CompilerParams(
            dimension_semantics=("arbitrary", "arbitrary", "arbitrary"), vmem_limit_bytes=VMEM_LIMIT),
        name="attn_prompt",
    )(bias2, proj, proj, proj, proj)


def _attn_sample_kernel(bc_ref, bn_ref, q_ref, kn_ref, vn_ref, g_ref, kc_ref, vc_ref, mix_ref, *, qscale, heads):
    win = kc_ref.shape[1] // heads

    def scores(h):
        sl = slice(h * LANES, (h + 1) * LANES)
        kc = kc_ref[0, pl.ds(h, win, stride=heads), :].astype(BF16)
        q16 = (q_ref[0, :, sl].astype(F32) * qscale).astype(BF16)
        return [_dot_nt(q16, kc) + bc_ref[h], _dot_nt(q16, kn_ref[0, :, sl].astype(BF16)) + bn_ref[h]]

    s_next = scores(0)
    for h in range(heads):
        sl = slice(h * LANES, (h + 1) * LANES)
        s_parts = s_next
        if h + 1 < heads:
            s_next = scores(h + 1)
        vc = vc_ref[0, pl.ds(h, win, stride=heads), :].astype(BF16)
        o = _softmax2_pv(s_parts, [vc, vn_ref[0, :, sl].astype(BF16)])
        mix_ref[0, :, sl] = (o * _silu(g_ref[0, :, sl].astype(F32))).astype(mix_ref.dtype)


def _attn_sample(proj, cache_k, cache_v, layer_j, bias_c2, bias_n2):
    nb, T, D4 = proj.shape
    D = D4 // 4
    H = HEADS
    W = cache_k.shape[2] // H
    new = lambda off: (lambda b: (b, 0, off))
    blk = (1, T, D)
    cblk = (None, 1, W * H, LANES)
    return pl.pallas_call(
        functools.partial(_attn_sample_kernel, qscale=float(LANES) ** -0.5 * LOG2E, heads=H),
        out_shape=jax.ShapeDtypeStruct((nb, T, D), BF16),
        grid=(nb,),
        in_specs=[pl.BlockSpec((H, T, W), lambda b: (0, 0, 0)),
                  pl.BlockSpec((H, T, T), lambda b: (0, 0, 0)),
                  pl.BlockSpec(blk, new(0)), pl.BlockSpec(blk, new(1)),
                  pl.BlockSpec(blk, new(2)), pl.BlockSpec(blk, new(3)),
                  pl.BlockSpec(cblk, lambda b: (layer_j, b, 0, 0)),
                  pl.BlockSpec(cblk, lambda b: (layer_j, b, 0, 0))],
        out_specs=pl.BlockSpec(blk, lambda b: (b, 0, 0)),
        compiler_params=pltpu.CompilerParams(
            dimension_semantics=("arbitrary",), vmem_limit_bytes=VMEM_LIMIT),
        name="attn_sample",
    )(bias_c2, bias_n2, proj, proj, proj, proj, cache_k, cache_v)


def _row_tile(m, cap):
    t = cap
    while m % t:
        t //= 2
    return t


def kernel(x_prompt, x_sample, state_hgrn, cache_attn_k, cache_attn_v, w_in, w_out, ln_g, ln_b,
           hgrn_lb, hgrn_norm_g, attn_rel_bias):
    B, S, D = x_prompt.shape
    nb, T, _ = x_sample.shape
    depth = w_in.shape[0]
    H = HEADS
    hd = D // H
    alpha = (2 * depth) ** 0.25
    kv_win = cache_attn_k.shape[2]
    prompt_win = min(ATTN_WINDOW, S)
    assert kv_win == ATTN_WINDOW

    w_out16 = w_out.astype(BF16)
    hp = x_prompt.reshape(B * S, D)
    hs = x_sample.reshape(nb * T, D)
    hp16, hs16 = hp, hs
    tm_p = _row_tile(B * S, 1024)
    tm_s = _row_tile(nb * T, 512)
    tn = 1024
    cache_k = cache_attn_k.reshape(cache_attn_k.shape[0], nb, kv_win * H, hd)
    cache_v = cache_attn_v.reshape(cache_attn_v.shape[0], nb, kv_win * H, hd)

    qc = (np.arange(ATTN_SUB)[:, None] + ATTN_WINDOW) // CHUNK
    kc = np.arange(ATTN_KW)[None, :] // CHUNK
    band_ok = jnp.asarray((kc <= qc) & (kc >= qc - ATTN_WINDOW // CHUNK))

    st_p, st_s, kp_rows, vp_rows, ks_rows, vs_rows = [], [], [], [], [], []
    for layer in range(depth):
        j = layer // N_MIXERS
        if layer % N_MIXERS == 0:
            def sections(x16, nseq, tm):
                sec = lambda y: y.reshape(nseq, -1, D)
                return (sec(_in_proj(x16, w_in, layer, 0, D, tm, tn, "silu")),
                        sec(_in_proj(x16, w_in, layer, D, D, tm, tn, "plain")),
                        sec(_in_proj(x16, w_in, layer, 2 * D, D, tm, tn, "plain", BF16)),
                        sec(_in_proj(x16, w_in, layer, 3 * D, D, tm, tn, "silu")))

            mix_p, s_fin = _hgrn_mixer(*sections(hp16, B, tm_p), hgrn_lb, hgrn_norm_g[j], j, min(HGRN_CHUNK, S),
                                       HGRN_HEADS_PROMPT)
            mix_s, s_new = _hgrn_mixer(*sections(hs16, nb, tm_s), hgrn_lb, hgrn_norm_g[j], j, T,
                                       HGRN_HEADS_SAMPLE, s0=state_hgrn[j])
            st_p.append(s_fin.astype(state_hgrn.dtype))
            st_s.append(s_new.astype(state_hgrn.dtype))
        else:
            proj_p = _in_proj(hp16, w_in, layer, 0, 4 * D, _row_tile(B * S, 2048), tn, "plain",
                              BF16).reshape(B, S, 4 * D)
            proj_s = _in_proj(hs16, w_in, layer, 0, 4 * D, tm_s, tn, "plain", BF16).reshape(nb, T, 4 * D)
            toep2 = _toeplitz_bias(attn_rel_bias[j].astype(F32), ATTN_SUB, ATTN_KW) * LOG2E
            mix_p = _attn_prompt(proj_p, jnp.swapaxes(jnp.where(band_ok, toep2, NEG_BIG), 1, 2))
            mix_s = _attn_sample(proj_s, cache_k, cache_v, j,
                                 toep2[:, :T, :kv_win], toep2[:, :T, kv_win:kv_win + T])
            out_dt = x_prompt.dtype
            kp_rows.append(proj_p[:, S - prompt_win:, D:2 * D].astype(out_dt).reshape(B, prompt_win, H, hd))
            vp_rows.append(proj_p[:, S - prompt_win:, 2 * D:3 * D].astype(out_dt).reshape(B, prompt_win, H, hd))
            ks_rows.append(proj_s[:, :, D:2 * D].astype(out_dt).reshape(nb, T, H, hd))
            vs_rows.append(proj_s[:, :, 2 * D:3 * D].astype(out_dt).reshape(nb, T, H, hd))
        hp, hp16 = _out_ln(mix_p.reshape(B * S, D), w_out16, layer, hp, ln_g[layer], ln_b[layer], alpha,
                           _row_tile(B * S, OUT_TM))
        hs, hs16 = _out_ln(mix_s.reshape(nb * T, D), w_out16, layer, hs, ln_g[layer], ln_b[layer], alpha,
                           _row_tile(nb * T, OUT_TM))
    return (hp.reshape(B, S, D), hs.reshape(nb, T, D),
            jnp.stack(st_p), jnp.stack(st_s),
            jnp.stack(kp_rows), jnp.stack(vp_rows), jnp.stack(ks_rows), jnp.stack(vs_rows))
```

```python
import functools

import numpy as np
import jax
import jax.numpy as jnp
from jax import lax
from jax.experimental import pallas as pl
from jax.experimental.pallas import tpu as pltpu

F32 = jnp.float32
BF16 = jnp.bfloat16

N_MIXERS = 2
HEADS = 16
CHUNK = 64
ATTN_WINDOW = 8 * CHUNK
MAX_REL = 256
LN_EPS = 1e-5
RMS_EPS = 1e-6
LB_TINY = 1e-30
NEG_BIG = -1e30

LANES = 128
SUBLANES = 8
BF16_ROWS = 16
HGRN_CHUNK = 128
HGRN_HEADS_PROMPT = 16
HGRN_HEADS_SAMPLE = 16
HGRN_GROUP = 1
ATTN_TQ = ATTN_WINDOW
ATTN_SUB = 128
ATTN_KW = ATTN_SUB + ATTN_WINDOW
ATTN_HALF = 256
ATTN_HEADS = 16
OUT_TM = 512
OUT_SUB = 128
VMEM_LIMIT = 48 * 1024 * 1024
VMEM_LIMIT_BIG = 58 * 1024 * 1024
LOG2E = 1.4426950408889634

_DIMS_NT = (((1,), (1,)), ((), ()))


def _silu(x):
    h = 0.5 * x
    return h * jnp.tanh(h) + h


def _dot_nt(x, y):
    return lax.dot_general(x, y, _DIMS_NT, preferred_element_type=F32)


def _forget_lower_bound(lb_ref, layer_j):
    lbp = lb_ref[...].astype(F32)
    pe = jnp.exp(lbp - jnp.max(lbp, axis=0, keepdims=True))
    prob = pe / jnp.sum(pe, axis=0, keepdims=True)
    cs = prob[0:1]
    for i in range(1, layer_j + 1):
        cs = cs + prob[i:i + 1]
    return cs - prob[0:1]


def _mm_kernel(x_ref, w_ref, o_ref, w16_ref, *, silu):
    @pl.when(pl.program_id(1) == 0)
    def _():
        w16_ref[...] = w_ref[...].astype(BF16)

    acc = jnp.dot(x_ref[...].astype(BF16), w16_ref[...], preferred_element_type=F32)
    o_ref[...] = (_silu(acc) if silu else acc).astype(o_ref.dtype)


def _in_proj(x16, w, layer, col0, n_cols, tm, tn, mode="plain", out_dtype=F32, col1=None):
    M, K = x16.shape
    assert M % tm == 0 and n_cols % tn == 0 and col0 % tn == 0
    cb0 = col0 // tn
    if col1 is None:
        w_col = lambda j: cb0 + j
    else:
        half = n_cols // (2 * tn)
        assert n_cols % (2 * tn) == 0 and col1 % tn == 0
        w_col = lambda j: jnp.where(j < half, cb0 + j, col1 // tn - half + j)
    return pl.pallas_call(
        functools.partial(_mm_kernel, silu=mode == "silu"),
        out_shape=jax.ShapeDtypeStruct((M, n_cols), out_dtype),
        grid=(n_cols // tn, M // tm),
        in_specs=[pl.BlockSpec((tm, K), lambda j, i: (i, 0)),
                  pl.BlockSpec((None, K, tn), lambda j, i: (layer, 0, w_col(j)))],
        out_specs=pl.BlockSpec((tm, tn), lambda j, i: (i, j)),
        scratch_shapes=[pltpu.VMEM((K, tn), BF16)],
        compiler_params=pltpu.CompilerParams(
            dimension_semantics=("arbitrary", "arbitrary"),
            vmem_limit_bytes=VMEM_LIMIT_BIG if tm > 1024 else VMEM_LIMIT),
        name="in_proj_" + mode,
    )(x16, w)


def _layer_norm_rows(y, g, b):
    mu = jnp.mean(y, axis=-1, keepdims=True)
    yc = y - mu
    var = jnp.mean(yc * yc, axis=-1, keepdims=True)
    return yc * lax.rsqrt(var + LN_EPS) * g + b


def _out_ln_kernel(m_ref, w_ref, h_ref, g_ref, b_ref, o_ref, o16_ref, *, alpha, sub):
    for r0 in range(0, m_ref.shape[0], sub):
        rows = slice(r0, r0 + sub)
        y = alpha * h_ref[rows] + jnp.dot(m_ref[rows], w_ref[...], preferred_element_type=F32)
        out = _layer_norm_rows(y, g_ref[...], b_ref[...])
        o_ref[rows] = out
        o16_ref[rows] = out.astype(BF16)


def _out_ln(mix16, w16, layer, h, g, b, alpha, tm):
    M, D = h.shape
    assert M % tm == 0
    row = lambda i: (i, 0)
    const = lambda i: (0, 0)
    return pl.pallas_call(
        functools.partial(_out_ln_kernel, alpha=alpha, sub=min(tm, OUT_SUB)),
        out_shape=(jax.ShapeDtypeStruct((M, D), F32), jax.ShapeDtypeStruct((M, D), BF16)),
        grid=(M // tm,),
        in_specs=[pl.BlockSpec((tm, D), row),
                  pl.BlockSpec((None, D, D), lambda i: (layer, 0, 0), pipeline_mode=pl.Buffered(1)),
                  pl.BlockSpec((tm, D), row), pl.BlockSpec((1, D), const), pl.BlockSpec((1, D), const)],
        out_specs=(pl.BlockSpec((tm, D), row), pl.BlockSpec((tm, D), row)),
        compiler_params=pltpu.CompilerParams(
            dimension_semantics=("arbitrary",), vmem_limit_bytes=VMEM_LIMIT),
        name="out_proj_ln",
    )(mix16, w16, h, g.reshape(1, D), b.reshape(1, D))


def _head(x, h):
    return x[:, h * LANES:(h + 1) * LANES]


def _hgrn_stage1(qs, z, lb, sub, C):
    W = qs.shape[1]
    heads = W // LANES
    pair = lambda x, y: [_dot_nt(_head(x, h), _head(y, h)) for h in range(heads)]
    lb_c = jnp.maximum(lb, LB_TINY)
    e = jnp.exp(-jnp.abs(z))
    pos = z >= 0
    inv = 1.0 / (1.0 + e)
    a = jnp.log2(jnp.where(pos, 1.0 + lb_c * e, e + lb_c) * inv)
    kk = (1.0 - lb) * (jnp.where(pos, e, 1.0) * inv)
    qs16 = qs.astype(BF16)
    kk16 = kk.astype(BF16)

    V = C // SUBLANES
    shape3 = (V, SUBLANES, W)
    p3 = a.reshape(shape3)
    qs3 = qs.reshape(shape3)
    kk3 = kk.reshape(shape3)
    parts = [pair(qs16, kk16)]
    for n in (1, 2, 4):
        if n == 1:
            bn = jnp.where((sub & 1) != 0, pltpu.roll(p3, 1, 1), p3)
        elif n == 2:
            bn = jnp.where(sub < 4, jnp.broadcast_to(p3[:, 1:2], shape3), jnp.broadcast_to(p3[:, 5:6], shape3))
        else:
            bn = jnp.broadcast_to(p3[:, 3:4], shape3)
        right = (sub & n) != 0
        ex = jnp.exp2(jnp.where(right, p3, bn - p3))
        qn = (qs3 * ex).reshape(C, W).astype(BF16)
        kn = (kk3 * ex).reshape(C, W).astype(BF16)
        parts.append(pair(qn, kn))
        p3 = p3 + jnp.where(right, bn, 0.0)
    p = p3.reshape(C, W)

    n = SUBLANES
    while n < C:
        q_rows, k_rows, p_rows = [], [], []
        for lo in range(0, C, 2 * n):
            mid, hi = lo + n, lo + 2 * n
            bn = p[mid - 1:mid]
            k_left = kk[lo:mid] * jnp.exp2(bn - p[lo:mid])
            q_right = qs[mid:hi] * jnp.exp2(p[mid:hi])
            if n >= BF16_ROWS:
                q_rows += [qs16[lo:mid], q_right.astype(BF16)]
                k_rows += [k_left.astype(BF16), kk16[mid:hi]]
            else:
                q_rows += [qs[lo:mid], q_right]
                k_rows += [k_left, kk[mid:hi]]
            p_rows += [p[lo:mid], p[mid:hi] + bn]
        parts.append(pair(jnp.concatenate(q_rows, axis=0).astype(BF16),
                          jnp.concatenate(k_rows, axis=0).astype(BF16)))
        p = jnp.concatenate(p_rows, axis=0)
        n *= 2

    b_end = p[C - 1:C, :]
    qe = (qs * jnp.exp2(p)).astype(BF16)
    kd = (kk * jnp.exp2(b_end - p)).astype(BF16)
    return parts, qe, kd, jnp.exp2(b_end)


def _hgrn_stage2(parts, qe, kd, dec, v16, scale, st_ref, store_mix, masks, C, h0):
    for i in range(qe.shape[1] // LANES):
        h = h0 + i
        amat = jnp.where(masks[0], parts[0][i], 0.0)
        for k in (1, 2, 3):
            amat = jnp.where(masks[k], parts[k][i], amat)
        n, k = SUBLANES, 4
        while n < C:
            rows = []
            for lo in range(0, C, 2 * n):
                mid, hi = lo + n, lo + 2 * n
                rows.append(amat[lo:mid])
                rows.append(jnp.where(masks[k][mid:hi], parts[k][i][mid:hi], amat[mid:hi]))
            amat = jnp.concatenate(rows, axis=0)
            n, k = 2 * n, k + 1

        st = st_ref[h]
        vh = _head(v16, i)
        o = jnp.dot(amat.astype(BF16), vh, preferred_element_type=F32) + _dot_nt(_head(qe, i), st.astype(BF16))
        st_ref[h] = st * _head(dec, i) + jnp.dot(vh.T, _head(kd, i), preferred_element_type=F32)
        ms = jnp.mean(o * o, axis=-1, keepdims=True)
        store_mix(h, o * lax.rsqrt(ms + RMS_EPS) * _head(scale, i))


def _hgrn_kernel(*refs, layer_j, chunk, heads, has_s0):
    if has_s0:
        lvl_ref, lb_ref, ng_ref, q_ref, z_ref, v_ref, g_ref, s0_ref, mix_ref, sfin_ref, st_ref = refs
    else:
        lvl_ref, lb_ref, ng_ref, q_ref, z_ref, v_ref, g_ref, mix_ref, sfin_ref, st_ref = refs
    C = chunk
    c = pl.program_id(2)

    @pl.when(c == 0)
    def _():
        for h in range(heads):
            if has_s0:
                st_ref[h] = s0_ref[0, h].T
            else:
                st_ref[h] = jnp.zeros((LANES, LANES), F32)

    ng = ng_ref[...].astype(F32)
    lvl = lvl_ref[...]
    n_levels = C.bit_length()
    masks = [lvl == k for k in range(n_levels)]
    group = min(HGRN_GROUP, heads)
    gw = group * LANES
    sub = lax.broadcasted_iota(jnp.int32, (1, SUBLANES, gw), 1)
    lb = _forget_lower_bound(lb_ref, layer_j)

    def store_mix(h, val):
        mix_ref[0, :, h * LANES:(h + 1) * LANES] = val.astype(mix_ref.dtype)

    pending = None
    for h0 in range(0, heads, group):
        cols = slice(h0 * LANES, h0 * LANES + gw)
        stage1 = _hgrn_stage1(q_ref[0, :, cols], z_ref[0, :, cols], lb[:, cols], sub, C)
        if pending is not None:
            _hgrn_stage2(*pending)
        pending = (*stage1, v_ref[0, :, cols], ng[:, cols] * g_ref[0, :, cols], st_ref, store_mix, masks, C, h0)
    _hgrn_stage2(*pending)

    @pl.when(c == pl.num_programs(2) - 1)
    def _():
        for h in range(heads):
            sfin_ref[0, h] = st_ref[h].T


def _level_ids(C):
    t = np.arange(C)[:, None]
    s = np.arange(C)[None, :]
    x = t ^ s
    lv = np.full((C, C), -1, np.int32)
    lv[x == 0] = 0
    n, k = 1, 1
    while n < C:
        lv[(x >= n) & (x < 2 * n) & (t > s)] = k
        n, k = 2 * n, k + 1
    return jnp.asarray(lv)


def _hgrn_mixer(qg, z, v16, hgrn_lb, norm_g, layer_j, chunk, heads, s0=None):
    nb, T, D = z.shape
    H = HEADS
    assert D // H == LANES and T % chunk == 0 and H % heads == 0 and chunk % BF16_ROWS == 0
    has_s0 = s0 is not None
    hg = H // heads
    w = heads * LANES
    tok_blk = pl.BlockSpec((1, chunk, w), lambda b, h, c: (b, c, h))
    gate_blk = pl.BlockSpec((1, chunk, w), lambda b, h, c: (b, c, hg + h))
    st_blk = pl.BlockSpec((1, heads, LANES, LANES), lambda b, h, c: (b, h, 0, 0))
    in_specs = [
        pl.BlockSpec((chunk, chunk), lambda b, h, c: (0, 0)),
        pl.BlockSpec((hgrn_lb.shape[0], w), lambda b, h, c: (0, h)),
        pl.BlockSpec((1, w), lambda b, h, c: (0, h)),
        tok_blk, tok_blk, tok_blk, gate_blk,
    ]
    args = [_level_ids(chunk), hgrn_lb, norm_g.reshape(1, D), qg, z, v16, qg]
    if has_s0:
        in_specs.append(st_blk)
        args.append(s0)
    return pl.pallas_call(
        functools.partial(_hgrn_kernel, layer_j=layer_j, chunk=chunk, heads=heads, has_s0=has_s0),
        out_shape=(jax.ShapeDtypeStruct((nb, T, D), BF16),
                   jax.ShapeDtypeStruct((nb, H, LANES, LANES), F32)),
        grid=(nb, hg, T // chunk),
        in_specs=in_specs,
        out_specs=(pl.BlockSpec((1, chunk, w), lambda b, h, c: (b, c, h)), st_blk),
        scratch_shapes=[pltpu.VMEM((heads, LANES, LANES), F32)],
        compiler_params=pltpu.CompilerParams(
            dimension_semantics=("arbitrary", "arbitrary", "arbitrary"), vmem_limit_bytes=VMEM_LIMIT),
        name="hgrn_sample" if has_s0 else "hgrn_prompt",
    )(*args)


def _toeplitz_bias(table, rows, cols):
    H, n_rel = table.shape
    assert n_rel == MAX_REL + CHUNK
    d_hi = ATTN_WINDOW + rows - 1
    d_lo = ATTN_WINDOW - (cols - 1)
    left = d_hi - MAX_REL
    right = -(CHUNK - 1) - d_lo
    assert left >= 0 and right >= 0
    g = jnp.pad(table[:, ::-1], ((0, 0), (left, right)), mode="edge")
    n = rows + cols - 1
    g = jnp.pad(g, ((0, 0), (0, 1)))
    flat = jnp.tile(g, (1, rows))[:, rows - 1: rows - 1 + rows * n]
    return flat.reshape(H, rows, n)[:, :, :cols]


def _softmax2_pv(scores, values):
    m = functools.reduce(jnp.maximum, [jnp.max(s, axis=-1, keepdims=True) for s in scores])
    ps = [jnp.exp2(s - m) for s in scores]
    den = functools.reduce(lambda x, y: x + y, [jnp.sum(p, axis=-1, keepdims=True) for p in ps])
    o = functools.reduce(lambda x, y: x + y,
                         [jnp.dot(p.astype(BF16), v, preferred_element_type=F32) for p, v in zip(ps, values)])
    return o * (1.0 / den)


def _attn_prompt_kernel(bias_ref, q_ref, k_ref, v_ref, g_ref, mix_ref, kprev_ref, vprev_ref, *, qscale, heads):
    i = pl.program_id(2)
    tq = q_ref.shape[1]

    @pl.when(i == 0)
    def _():
        kprev_ref[...] = jnp.zeros_like(kprev_ref)
        vprev_ref[...] = jnp.zeros_like(vprev_ref)

    has_prev = i > 0

    halves = range(0, tq, ATTN_HALF)
    kwin = ATTN_HALF + ATTN_WINDOW

    def scores(h):
        sl = slice(h * LANES, (h + 1) * LANES)
        q16 = (q_ref[0, :, sl].astype(F32) * qscale).astype(BF16)
        kcat = jnp.concatenate([kprev_ref[:, sl], k_ref[0, :, sl].astype(BF16)], axis=0)
        return [_dot_nt(kcat[c0:c0 + kwin], q16[c0:c0 + ATTN_HALF]) for c0 in halves]

    s_next = scores(0)
    for h in range(heads):
        sl = slice(h * LANES, (h + 1) * LANES)
        s_halves = s_next
        if h + 1 < heads:
            s_next = scores(h + 1)
        bias = bias_ref[h]
        vcat_t = jnp.concatenate([vprev_ref[:, sl], v_ref[0, :, sl].astype(BF16)], axis=0).T
        o_cols = []
        for c0, s_half in zip(halves, s_halves):
            p_cols, inv_cols = [], []
            for r in range(0, ATTN_HALF, ATTN_SUB):
                n_prev = ATTN_WINDOW - c0 - r
                s = s_half[r:r + ATTN_KW, r:r + ATTN_SUB] + bias
                s = jnp.concatenate([jnp.where(has_prev, s[:n_prev], NEG_BIG), s[n_prev:]], axis=0)
                p = jnp.exp2(s - jnp.max(s, axis=0, keepdims=True))
                inv_cols.append(1.0 / jnp.sum(p, axis=0, keepdims=True))
                pieces = [p.astype(BF16)]
                if r:
                    pieces.insert(0, jnp.zeros((r, ATTN_SUB), BF16))
                if kwin - ATTN_KW - r:
                    pieces.append(jnp.zeros((kwin - ATTN_KW - r, ATTN_SUB), BF16))
                p_cols.append(jnp.concatenate(pieces, axis=0))
            o_half = jnp.dot(vcat_t[:, c0:c0 + kwin], jnp.concatenate(p_cols, axis=1), preferred_element_type=F32)
            o_cols.append(o_half * jnp.concatenate(inv_cols, axis=1))
        o = jnp.concatenate(o_cols, axis=1).T
        mix_ref[0, :, sl] = (o * _silu(g_ref[0, :, sl].astype(F32))).astype(mix_ref.dtype)

    kprev_ref[...] = k_ref[0].astype(BF16)
    vprev_ref[...] = v_ref[0].astype(BF16)


def _attn_prompt(proj, bias2):
    B, S, D4 = proj.shape
    D = D4 // 4
    H = HEADS
    tq = ATTN_TQ
    heads = ATTN_HEADS
    hg = H // heads
    w = heads * LANES
    assert D // H == LANES and S % tq == 0 and H % heads == 0
    col = lambda off: (lambda b, h, i: (b, i, off * hg + h))
    blk = (1, tq, w)
    return pl.pallas_call(
        functools.partial(_attn_prompt_kernel, qscale=float(LANES) ** -0.5 * LOG2E, heads=heads),
        out_shape=jax.ShapeDtypeStruct((B, S, D), BF16),
        grid=(B, hg, S // tq),
        in_specs=[pl.BlockSpec((heads, ATTN_KW, ATTN_SUB), lambda b, h, i: (h, 0, 0)),
                  pl.BlockSpec(blk, col(0)), pl.BlockSpec(blk, col(1)),
                  pl.BlockSpec(blk, col(2)), pl.BlockSpec(blk, col(3))],
        out_specs=pl.BlockSpec(blk, lambda b, h, i: (b, i, h)),
        scratch_shapes=[pltpu.VMEM((tq, w), BF16), pltpu.VMEM((tq, w), BF16)],
        compiler_params=pltpu.CompilerParams(
            dimension_semantics=("arbitrary", "arbitrary", "arbitrary"), vmem_limit_bytes=VMEM_LIMIT),
        name="attn_prompt",
    )(bias2, proj, proj, proj, proj)


def _attn_sample_kernel(bc_ref, bn_ref, q_ref, kn_ref, vn_ref, g_ref, kc_ref, vc_ref, mix_ref, *, qscale, heads):
    win = kc_ref.shape[1] // heads

    def scores(h):
        sl = slice(h * LANES, (h + 1) * LANES)
        kc = kc_ref[0, pl.ds(h, win, stride=heads), :].astype(BF16)
        q16 = (q_ref[0, :, sl].astype(F32) * qscale).astype(BF16)
        return [_dot_nt(q16, kc) + bc_ref[h], _dot_nt(q16, kn_ref[0, :, sl].astype(BF16)) + bn_ref[h]]

    s_next = scores(0)
    for h in range(heads):
        sl = slice(h * LANES, (h + 1) * LANES)
        s_parts = s_next
        if h + 1 < heads:
            s_next = scores(h + 1)
        vc = vc_ref[0, pl.ds(h, win, stride=heads), :].astype(BF16)
        o = _softmax2_pv(s_parts, [vc, vn_ref[0, :, sl].astype(BF16)])
        mix_ref[0, :, sl] = (o * _silu(g_ref[0, :, sl].astype(F32))).astype(mix_ref.dtype)


def _attn_sample(proj, cache_k, cache_v, layer_j, bias_c2, bias_n2):
    nb, T, D4 = proj.shape
    D = D4 // 4
    H = HEADS
    W = cache_k.shape[2] // H
    new = lambda off: (lambda b: (b, 0, off))
    blk = (1, T, D)
    cblk = (None, 1, W * H, LANES)
    return pl.pallas_call(
        functools.partial(_attn_sample_kernel, qscale=float(LANES) ** -0.5 * LOG2E, heads=H),
        out_shape=jax.ShapeDtypeStruct((nb, T, D), BF16),
        grid=(nb,),
        in_specs=[pl.BlockSpec((H, T, W), lambda b: (0, 0, 0)),
                  pl.BlockSpec((H, T, T), lambda b: (0, 0, 0)),
                  pl.BlockSpec(blk, new(0)), pl.BlockSpec(blk, new(1)),
                  pl.BlockSpec(blk, new(2)), pl.BlockSpec(blk, new(3)),
                  pl.BlockSpec(cblk, lambda b: (layer_j, b, 0, 0)),
                  pl.BlockSpec(cblk, lambda b: (layer_j, b, 0, 0))],
        out_specs=pl.BlockSpec(blk, lambda b: (b, 0, 0)),
        compiler_params=pltpu.CompilerParams(
            dimension_semantics=("arbitrary",), vmem_limit_bytes=VMEM_LIMIT),
        name="attn_sample",
    )(bias_c2, bias_n2, proj, proj, proj, proj, cache_k, cache_v)


def _row_tile(m, cap):
    t = cap
    while m % t:
        t //= 2
    return t


def kernel(x_prompt, x_sample, state_hgrn, cache_attn_k, cache_attn_v, w_in, w_out, ln_g, ln_b,
           hgrn_lb, hgrn_norm_g, attn_rel_bias):
    B, S, D = x_prompt.shape
    nb, T, _ = x_sample.shape
    depth = w_in.shape[0]
    H = HEADS
    hd = D // H
    alpha = (2 * depth) ** 0.25
    kv_win = cache_attn_k.shape[2]
    prompt_win = min(ATTN_WINDOW, S)
    assert kv_win == ATTN_WINDOW

    w_out16 = w_out.astype(BF16)
    hp = x_prompt.reshape(B * S, D)
    hs = x_sample.reshape(nb * T, D)
    hp16, hs16 = hp, hs
    tm_p = _row_tile(B * S, 1024)
    tm_s = _row_tile(nb * T, 512)
    tn = 1024
    cache_k = cache_attn_k.reshape(cache_attn_k.shape[0], nb, kv_win * H, hd)
    cache_v = cache_attn_v.reshape(cache_attn_v.shape[0], nb, kv_win * H, hd)

    qc = (np.arange(ATTN_SUB)[:, None] + ATTN_WINDOW) // CHUNK
    kc = np.arange(ATTN_KW)[None, :] // CHUNK
    band_ok = jnp.asarray((kc <= qc) & (kc >= qc - ATTN_WINDOW // CHUNK))

    st_p, st_s, kp_rows, vp_rows, ks_rows, vs_rows = [], [], [], [], [], []
    for layer in range(depth):
        j = layer // N_MIXERS
        if layer % N_MIXERS == 0:
            def sections(x16, nseq, tm):
                sec = lambda y: y.reshape(nseq, -1, y.shape[-1])
                return (sec(_in_proj(x16, w_in, layer, 0, 2 * D, tm, tn, "silu", col1=3 * D)),
                        sec(_in_proj(x16, w_in, layer, D, D, tm, tn, "plain")),
                        sec(_in_proj(x16, w_in, layer, 2 * D, D, tm, tn, "plain", BF16)))

            mix_p, s_fin = _hgrn_mixer(*sections(hp16, B, tm_p), hgrn_lb, hgrn_norm_g[j], j, min(HGRN_CHUNK, S),
                                       HGRN_HEADS_PROMPT)
            mix_s, s_new = _hgrn_mixer(*sections(hs16, nb, tm_s), hgrn_lb, hgrn_norm_g[j], j, T,
                                       HGRN_HEADS_SAMPLE, s0=state_hgrn[j])
            st_p.append(s_fin.astype(state_hgrn.dtype))
            st_s.append(s_new.astype(state_hgrn.dtype))
        else:
            proj_p = _in_proj(hp16, w_in, layer, 0, 4 * D, _row_tile(B * S, 2048), tn, "plain",
                              BF16).reshape(B, S, 4 * D)
            proj_s = _in_proj(hs16, w_in, layer, 0, 4 * D, tm_s, tn, "plain", BF16).reshape(nb, T, 4 * D)
            toep2 = _toeplitz_bias(attn_rel_bias[j].astype(F32), ATTN_SUB, ATTN_KW) * LOG2E
            mix_p = _attn_prompt(proj_p, jnp.swapaxes(jnp.where(band_ok, toep2, NEG_BIG), 1, 2))
            mix_s = _attn_sample(proj_s, cache_k, cache_v, j,
                                 toep2[:, :T, :kv_win], toep2[:, :T, kv_win:kv_win + T])
            out_dt = x_prompt.dtype
            kp_rows.append(proj_p[:, S - prompt_win:, D:2 * D].astype(out_dt).reshape(B, prompt_win, H, hd))
            vp_rows.append(proj_p[:, S - prompt_win:, 2 * D:3 * D].astype(out_dt).reshape(B, prompt_win, H, hd))
            ks_rows.append(proj_s[:, :, D:2 * D].astype(out_dt).reshape(nb, T, H, hd))
            vs_rows.append(proj_s[:, :, 2 * D:3 * D].astype(out_dt).reshape(nb, T, H, hd))
        hp, hp16 = _out_ln(mix_p.reshape(B * S, D), w_out16, layer, hp, ln_g[layer], ln_b[layer], alpha,
                           _row_tile(B * S, OUT_TM))
        hs, hs16 = _out_ln(mix_s.reshape(nb * T, D), w_out16, layer, hs, ln_g[layer], ln_b[layer], alpha,
                           _row_tile(nb * T, OUT_TM))
    return (hp.reshape(B, S, D), hs.reshape(nb, T, D),
            jnp.stack(st_p), jnp.stack(st_s),
            jnp.stack(kp_rows), jnp.stack(vp_rows), jnp.stack(ks_rows), jnp.stack(vs_rows))
```
